```python
import math
import jax, jax.numpy as jnp
from jax import lax
import numpy as np


D_MODEL = 2048
BATCH = 4
SEQ = 8192
DEPTH = 2

GRID_W = 64
HEAD_DIM = 128
N_HEADS = D_MODEL // HEAD_DIM
N_KV_HEADS = N_HEADS // 4
GQA_GROUP = N_HEADS // N_KV_HEADS
ATTN_WIDTH = N_HEADS * HEAD_DIM
KV_WIDTH = N_KV_HEADS * HEAD_DIM
ATTN_IN_WIDTH = ATTN_WIDTH + 2 * KV_WIDTH + ATTN_WIDTH
ROPE_AXIS_DIM = HEAD_DIM // 2
ROPE_THETA = 10000.0
Q_BLOCK = 128
FOURIER_WIDTH = D_MODEL
FOURIER_GROUPS = 8
FOURIER_GROUP_W = FOURIER_WIDTH // FOURIER_GROUPS
FOURIER_IN_WIDTH = 2 * FOURIER_WIDTH
N_MIXERS = 2
N_ATTN_LAYERS = (DEPTH + 1) // 2
N_FOURIER_LAYERS = DEPTH // 2
EPS = 1e-6

kernel_name = "hybrid_gqa_axial_rope_fnet_adaln_encoder"


def rms_norm(x, gain):
    x32 = x.astype(jnp.float32)
    y = x32 * lax.rsqrt(jnp.mean(x32 * x32, axis=-1, keepdims=True) + EPS)
    return y.astype(x.dtype) * gain


def axial_rope_tables(seq_len):
    rows = seq_len // GRID_W
    row_ids = jnp.repeat(jnp.arange(rows), GRID_W).astype(jnp.float32)
    col_ids = jnp.tile(jnp.arange(GRID_W), rows).astype(jnp.float32)
    inv_freq = ROPE_THETA ** (-jnp.arange(0, ROPE_AXIS_DIM, 2, dtype=jnp.float32) / ROPE_AXIS_DIM)
    ang = jnp.concatenate([row_ids[:, None] * inv_freq[None, :],
                           col_ids[:, None] * inv_freq[None, :]], axis=-1)
    return jnp.cos(ang), jnp.sin(ang)


def apply_rope(x, cos, sin):
    xf = x.astype(jnp.float32).reshape(*x.shape[:-1], HEAD_DIM // 2, 2)
    x1, x2 = xf[..., 0], xf[..., 1]
    c = cos[None, :, None, :]
    s = sin[None, :, None, :]
    out = jnp.stack([x1 * c - x2 * s, x1 * s + x2 * c], axis=-1)
    return out.reshape(x.shape).astype(x.dtype)


def attention_mixer(h, w_in, q_gain, k_gain, w_out):
    B, S, _ = h.shape
    proj = h @ w_in
    q = proj[..., :ATTN_WIDTH].reshape(B, S, N_HEADS, HEAD_DIM)
    k = proj[..., ATTN_WIDTH:ATTN_WIDTH + KV_WIDTH].reshape(B, S, N_KV_HEADS, HEAD_DIM)
    v = proj[..., ATTN_WIDTH + KV_WIDTH:ATTN_WIDTH + 2 * KV_WIDTH].reshape(B, S, N_KV_HEADS, HEAD_DIM)
    gate = proj[..., ATTN_WIDTH + 2 * KV_WIDTH:]
    q = rms_norm(q, q_gain)
    k = rms_norm(k, k_gain)
    cos, sin = axial_rope_tables(S)
    q = apply_rope(q, cos, sin)
    k = apply_rope(k, cos, sin)
    scale = 1.0 / math.sqrt(HEAD_DIM)
    k32 = k.astype(jnp.float32)
    n_blocks = S // Q_BLOCK
    qb = q.reshape(B, n_blocks, Q_BLOCK, N_KV_HEADS, GQA_GROUP, HEAD_DIM).transpose(1, 0, 2, 3, 4, 5)

    def attend(q_blk):
        s = jnp.einsum('bqkgd,bskd->bkgqs', q_blk.astype(jnp.float32), k32) * scale
        p = jax.nn.softmax(s, axis=-1)
        return jnp.einsum('bkgqs,bskd->bqkgd', p.astype(v.dtype), v)

    o = lax.map(attend, qb)
    o = o.transpose(1, 0, 2, 3, 4, 5).reshape(B, S, ATTN_WIDTH)
    return (o * jax.nn.silu(gate)) @ w_out


def fourier_mixer(h, w_in, w_out):
    B, S, _ = h.shape
    proj = h @ w_in
    u = proj[..., :FOURIER_WIDTH]
    gate = proj[..., FOURIER_WIDTH:]
    ug = u.astype(jnp.float32).reshape(B, S, FOURIER_GROUPS, FOURIER_GROUP_W)
    f = jnp.fft.fft2(ug, axes=(1, 3), norm="ortho").real
    f = f.reshape(B, S, FOURIER_WIDTH).astype(h.dtype)
    return (f * jax.nn.silu(gate)) @ w_out


def setup_inputs(seed: int = 0) -> dict:
    key = jax.random.key(seed)
    ks = jax.random.split(key, 12)
    f32 = jnp.float32
    D = D_MODEL
    x = jax.random.normal(ks[0], (BATCH, SEQ, D), f32)
    c = jax.random.normal(ks[1], (BATCH, D), f32)
    norm_g = 1.0 + 0.02 * jax.random.normal(ks[2], (DEPTH, D), f32)
    ada_w = jax.random.normal(ks[3], (DEPTH, D, 3 * D), f32) * D ** -0.5
    ada_b = 0.02 * jax.random.normal(ks[4], (DEPTH, 3 * D), f32)
    attn_w_in = jax.random.normal(ks[5], (N_ATTN_LAYERS, D, ATTN_IN_WIDTH), f32) * D ** -0.5
    attn_q_gain = 1.0 + 0.02 * jax.random.normal(ks[6], (N_ATTN_LAYERS, HEAD_DIM), f32)
    attn_k_gain = 1.0 + 0.02 * jax.random.normal(ks[7], (N_ATTN_LAYERS, HEAD_DIM), f32)
    attn_w_out = jax.random.normal(ks[8], (N_ATTN_LAYERS, ATTN_WIDTH, D), f32) * ATTN_WIDTH ** -0.5
    fourier_w_in = jax.random.normal(ks[9], (N_FOURIER_LAYERS, D, FOURIER_IN_WIDTH), f32) * D ** -0.5
    fourier_w_out = jax.random.normal(ks[10], (N_FOURIER_LAYERS, FOURIER_WIDTH, D), f32) * FOURIER_WIDTH ** -0.5
    final_g = 1.0 + 0.02 * jax.random.normal(ks[11], (D,), f32)
    return {"x": x, "c": c, "norm_g": norm_g, "ada_w": ada_w, "ada_b": ada_b,
            "attn_w_in": attn_w_in, "attn_q_gain": attn_q_gain, "attn_k_gain": attn_k_gain,
            "attn_w_out": attn_w_out, "fourier_w_in": fourier_w_in, "fourier_w_out": fourier_w_out,
            "final_g": final_g}


def reference(x, c, norm_g, ada_w, ada_b, attn_w_in, attn_q_gain, attn_k_gain,
              attn_w_out, fourier_w_in, fourier_w_out, final_g):
    D = D_MODEL
    c_act = jax.nn.silu(c)
    for i in range(DEPTH):
        mod = c_act @ ada_w[i] + ada_b[i]
        shift = mod[:, None, :D]
        scale = mod[:, None, D:2 * D]
        gate = mod[:, None, 2 * D:]
        h = rms_norm(x, norm_g[i]) * (1.0 + scale) + shift
        j = i // N_MIXERS
        if i % N_MIXERS == 0:
            y = attention_mixer(h, attn_w_in[j], attn_q_gain[j], attn_k_gain[j], attn_w_out[j])
        else:
            y = fourier_mixer(h, fourier_w_in[j], fourier_w_out[j])
        x = x + gate * y
    return rms_norm(x, final_g)
```

```python
import math

import numpy as np
import jax
import jax.numpy as jnp
from jax import lax
from jax.experimental import pallas as pl
from jax.experimental.pallas import tpu as pltpu

D_MODEL = 2048
BATCH = 4
SEQ = 8192
GRID_W = 64
HEAD_DIM = 128
N_HEADS = 16
N_KV_HEADS = 4
GQA_GROUP = N_HEADS // N_KV_HEADS
ATTN_WIDTH = N_HEADS * HEAD_DIM
KV_WIDTH = N_KV_HEADS * HEAD_DIM
ATTN_IN_WIDTH = 2 * ATTN_WIDTH + 2 * KV_WIDTH
ROPE_AXIS_DIM = HEAD_DIM // 2
ROPE_THETA = 10000.0
FOURIER_GROUPS = 8
FOURIER_GROUP_W = D_MODEL // FOURIER_GROUPS
EPS = 1e-6

DFT_N1 = 128
DFT_N2 = SEQ // DFT_N1

F32 = jnp.float32
BF16 = jnp.bfloat16

VMEM_LIMIT_BYTES = 56 * 1024 * 1024

ROW_TILE = 512
COL_TILE = 512
ADALN_COL_TILE = 1024
Q_TILE = 256
KV_TILE = 512
DFT1_COL_TILE = 4096
K1_TILE = 8


def _params(*semantics):
    return pltpu.CompilerParams(dimension_semantics=semantics,
                                vmem_limit_bytes=VMEM_LIMIT_BYTES)


def _silu(x):
    return x * jax.nn.sigmoid(x)


def _adaln_body(c_ref, w_ref, b_ref, o_ref):
    c_act = _silu(c_ref[...])
    o_ref[0] = jnp.dot(c_act, w_ref[0], preferred_element_type=F32,
                       precision=lax.Precision.HIGHEST) + b_ref[0]


def _adaln(c_pad, ada_w, ada_b):
    depth, d, n = ada_w.shape
    rows = c_pad.shape[0]
    return pl.pallas_call(
        _adaln_body,
        grid=(depth, n // ADALN_COL_TILE),
        in_specs=[
            pl.BlockSpec((rows, d), lambda l, j: (0, 0)),
            pl.BlockSpec((1, d, ADALN_COL_TILE), lambda l, j: (l, 0, j)),
            pl.BlockSpec((1, 1, ADALN_COL_TILE), lambda l, j: (l, 0, j)),
        ],
        out_specs=pl.BlockSpec((1, rows, ADALN_COL_TILE), lambda l, j: (l, 0, j)),
        out_shape=jax.ShapeDtypeStruct((depth, rows, n), F32),
        compiler_params=_params("arbitrary", "arbitrary"),
        name="adaln",
    )(c_pad, ada_w, ada_b.reshape(depth, 1, n))


def _modulated_norm(x_ref, g_ref, sc_ref, sh_ref):
    x = x_ref[...]
    ms = jnp.mean(x * x, axis=-1, keepdims=True)
    a = g_ref[...] * (1.0 + sc_ref[0])
    return (x * lax.rsqrt(ms + EPS) * a + sh_ref[0]).astype(BF16)


def _attn_inproj_body(x_ref, g_ref, sc_ref, sh_ref, w_ref, gain_ref, cos_ref, sin_ref,
                      o_ref, h_scr):
    j = pl.program_id(1)

    @pl.when(j == 0)
    def _():
        h_scr[...] = _modulated_norm(x_ref, g_ref, sc_ref, sh_ref)

    acc = jnp.dot(h_scr[...], w_ref[...], preferred_element_type=F32)
    n_qk_tiles = (ATTN_WIDTH + KV_WIDTH) // COL_TILE

    @pl.when(j < n_qk_tiles)
    def _():
        cos = cos_ref[...]
        sin = sin_ref[...]
        for hh in range(COL_TILE // HEAD_DIM):
            sl = slice(hh * HEAD_DIM, (hh + 1) * HEAD_DIM)
            xh = acc[:, sl]
            ms = jnp.mean(xh * xh, axis=-1, keepdims=True)
            xn = xh * lax.rsqrt(ms + EPS) * gain_ref[:, sl]
            rot = xn * cos + pltpu.roll(xn, HEAD_DIM // 2, axis=1) * sin
            o_ref[:, sl] = rot.astype(BF16)

    @pl.when(j == n_qk_tiles)
    def _():
        o_ref[...] = acc.astype(BF16)

    @pl.when(j > n_qk_tiles)
    def _():
        o_ref[...] = _silu(acc).astype(BF16)


def _attn_inproj(x2d, norm_g, scale, shift, w_bf16, gain_vec, cos_full, sin_signed):
    t, d = x2d.shape
    n = w_bf16.shape[1]
    tiles_per_batch = SEQ // ROW_TILE
    return pl.pallas_call(
        _attn_inproj_body,
        grid=(t // ROW_TILE, n // COL_TILE),
        in_specs=[
            pl.BlockSpec((ROW_TILE, d), lambda i, j: (i, 0)),
            pl.BlockSpec((1, d), lambda i, j: (0, 0)),
            pl.BlockSpec((1, 1, d), lambda i, j: (i // tiles_per_batch, 0, 0)),
            pl.BlockSpec((1, 1, d), lambda i, j: (i // tiles_per_batch, 0, 0)),
            pl.BlockSpec((d, COL_TILE), lambda i, j: (0, j)),
            pl.BlockSpec((1, COL_TILE), lambda i, j: (0, j)),
            pl.BlockSpec((ROW_TILE, HEAD_DIM), lambda i, j: (i % tiles_per_batch, 0)),
            pl.BlockSpec((ROW_TILE, HEAD_DIM), lambda i, j: (i % tiles_per_batch, 0)),
        ],
        out_specs=pl.BlockSpec((ROW_TILE, COL_TILE), lambda i, j: (i, j)),
        out_shape=jax.ShapeDtypeStruct((t, n), BF16),
        scratch_shapes=[pltpu.VMEM((ROW_TILE, d), BF16)],
        compiler_params=_params("arbitrary", "arbitrary"),
        name="attn_inproj",
    )(x2d, norm_g, scale, shift, w_bf16, gain_vec, cos_full, sin_signed)


def _attn_body(q_ref, k_ref, v_ref, sg_ref, o_ref):
    n_kv_steps = SEQ // KV_TILE
    for hh in range(GQA_GROUP):
        sl = slice(hh * HEAD_DIM, (hh + 1) * HEAD_DIM)
        q = q_ref[0, :, sl]

        def step(c, carry, q=q):
            m, l, acc = carry
            start = pl.multiple_of(c * KV_TILE, KV_TILE)
            kc = k_ref[0, pl.ds(start, KV_TILE), :]
            vc = v_ref[0, pl.ds(start, KV_TILE), :]
            s = lax.dot_general(q, kc, (((1,), (1,)), ((), ())),
                                preferred_element_type=F32)
            m_new = jnp.maximum(m, jnp.max(s, axis=-1, keepdims=True))
            alpha = jnp.exp(m - m_new)
            p = jnp.exp(s - m_new)
            l = alpha * l + jnp.sum(p, axis=-1, keepdims=True)
            acc = alpha * acc + jnp.dot(p.astype(BF16), vc, preferred_element_type=F32)
            return m_new, l, acc

        init = (jnp.full((Q_TILE, 1), -jnp.inf, F32), jnp.zeros((Q_TILE, 1), F32),
                jnp.zeros((Q_TILE, HEAD_DIM), F32))
        _, l, acc = lax.fori_loop(0, n_kv_steps, step, init)
        o_ref[0, :, sl] = ((acc / l) * sg_ref[0, :, sl].astype(F32)).astype(BF16)


def _attention(proj3d):
    b, s, _ = proj3d.shape
    group_w = GQA_GROUP * HEAD_DIM
    k_blk0 = ATTN_WIDTH // HEAD_DIM
    v_blk0 = (ATTN_WIDTH + KV_WIDTH) // HEAD_DIM
    g_blk0 = (ATTN_WIDTH + 2 * KV_WIDTH) // group_w
    return pl.pallas_call(
        _attn_body,
        grid=(b, N_KV_HEADS, s // Q_TILE),
        in_specs=[
            pl.BlockSpec((1, Q_TILE, group_w), lambda bi, kh, qi: (bi, qi, kh)),
            pl.BlockSpec((1, s, HEAD_DIM), lambda bi, kh, qi: (bi, 0, k_blk0 + kh)),
            pl.BlockSpec((1, s, HEAD_DIM), lambda bi, kh, qi: (bi, 0, v_blk0 + kh)),
            pl.BlockSpec((1, Q_TILE, group_w), lambda bi, kh, qi: (bi, qi, g_blk0 + kh)),
        ],
        out_specs=pl.BlockSpec((1, Q_TILE, group_w), lambda bi, kh, qi: (bi, qi, kh)),
        out_shape=jax.ShapeDtypeStruct((b, s, ATTN_WIDTH), BF16),
        compiler_params=_params("arbitrary", "arbitrary", "arbitrary"),
        name="attention",
    )(proj3d, proj3d, proj3d, proj3d)


def _outproj_body(a_ref, w_ref, x_ref, gm_ref, o_ref):
    y = jnp.dot(a_ref[...], w_ref[...], preferred_element_type=F32)
    o_ref[...] = x_ref[...] + gm_ref[0] * y


def _outproj_residual(a2d, w_bf16, x2d, gate_mod):
    t, d = x2d.shape
    tiles_per_batch = SEQ // ROW_TILE
    return pl.pallas_call(
        _outproj_body,
        grid=(t // ROW_TILE,),
        in_specs=[
            pl.BlockSpec((ROW_TILE, a2d.shape[1]), lambda i: (i, 0)),
            pl.BlockSpec(w_bf16.shape, lambda i: (0, 0)),
            pl.BlockSpec((ROW_TILE, d), lambda i: (i, 0)),
            pl.BlockSpec((1, 1, d), lambda i: (i // tiles_per_batch, 0, 0)),
        ],
        out_specs=pl.BlockSpec((ROW_TILE, d), lambda i: (i, 0)),
        out_shape=jax.ShapeDtypeStruct((t, d), F32),
        compiler_params=_params("arbitrary"),
        name="attn_outproj",
    )(a2d, w_bf16, x2d, gate_mod)


def _fourier_inproj_body(x_ref, g_ref, sc_ref, sh_ref, w_ref, cs_ref, z_ref, sg_ref, h_scr):
    j = pl.program_id(1)
    n_u_tiles = D_MODEL // COL_TILE

    @pl.when(j == 0)
    def _():
        h_scr[...] = _modulated_norm(x_ref, g_ref, sc_ref, sh_ref)

    acc = jnp.dot(h_scr[...], w_ref[...], preferred_element_type=F32)

    @pl.when(j < n_u_tiles)
    def _():
        for gg in range(COL_TILE // FOURIER_GROUP_W):
            sl = slice(gg * FOURIER_GROUP_W, (gg + 1) * FOURIER_GROUP_W)
            z = jnp.dot(acc[:, sl].astype(BF16), cs_ref[...], preferred_element_type=F32)
            z_ref[0, 0, :, sl] = z[:, :FOURIER_GROUP_W].astype(BF16)
            z_ref[0, 1, :, sl] = z[:, FOURIER_GROUP_W:].astype(BF16)

    @pl.when(j >= n_u_tiles)
    def _():
        sg_ref[0] = _silu(acc)


def _fourier_inproj(x2d, norm_g, scale, shift, w_bf16, cs):
    t, d = x2d.shape
    n = w_bf16.shape[1]
    tiles_per_batch = SEQ // ROW_TILE
    n_u_tiles = D_MODEL // COL_TILE
    return pl.pallas_call(
        _fourier_inproj_body,
        grid=(t // ROW_TILE, n // COL_TILE),
        in_specs=[
            pl.BlockSpec((ROW_TILE, d), lambda i, j: (i, 0)),
            pl.BlockSpec((1, d), lambda i, j: (0, 0)),
            pl.BlockSpec((1, 1, d), lambda i, j: (i // tiles_per_batch, 0, 0)),
            pl.BlockSpec((1, 1, d), lambda i, j: (i // tiles_per_batch, 0, 0)),
            pl.BlockSpec((d, COL_TILE), lambda i, j: (0, j)),
            pl.BlockSpec(cs.shape, lambda i, j: (0, 0)),
        ],
        out_specs=[
            pl.BlockSpec((1, 2, ROW_TILE, COL_TILE),
                         lambda i, j: (i // tiles_per_batch, 0, i % tiles_per_batch,
                                       jnp.minimum(j, n_u_tiles - 1))),
            pl.BlockSpec((1, ROW_TILE, COL_TILE),
                         lambda i, j: (i // tiles_per_batch, i % tiles_per_batch,
                                       jnp.maximum(j - n_u_tiles, 0))),
        ],
        out_shape=[jax.ShapeDtypeStruct((BATCH, 2, SEQ, D_MODEL), BF16),
                   jax.ShapeDtypeStruct((BATCH, SEQ, D_MODEL), F32)],
        scratch_shapes=[pltpu.VMEM((ROW_TILE, d), BF16)],
        compiler_params=_params("arbitrary", "arbitrary"),
        name="fourier_inproj",
    )(x2d, norm_g, scale, shift, w_bf16, cs)


def _dft1_body(m_ref, x_ref, o_ref):
    o_ref[0] = jnp.dot(m_ref[...], x_ref[0], preferred_element_type=F32).astype(BF16)


def _dft_stage1(m1, z3d):
    b, rows, cols = z3d.shape
    return pl.pallas_call(
        _dft1_body,
        grid=(b, cols // DFT1_COL_TILE),
        in_specs=[
            pl.BlockSpec(m1.shape, lambda bi, j: (0, 0)),
            pl.BlockSpec((1, rows, DFT1_COL_TILE), lambda bi, j: (bi, 0, j)),
        ],
        out_specs=pl.BlockSpec((1, rows, DFT1_COL_TILE), lambda bi, j: (bi, 0, j)),
        out_shape=jax.ShapeDtypeStruct((b, rows, cols), BF16),
        compiler_params=_params("arbitrary", "arbitrary"),
        name="dft_stage1",
    )(m1, z3d)


def _dft2_body(ar_ref, ai_ref, tc_ref, ts_ref, sg_ref, o_ref):
    for k in range(K1_TILE):
        f = (jnp.dot(tc_ref[k], ar_ref[0, k], preferred_element_type=F32)
             + jnp.dot(ts_ref[k], ai_ref[0, k], preferred_element_type=F32))
        o_ref[0, k] = (f * sg_ref[0, :, k, :]).astype(BF16)


def _dft_stage2_gate(a4d, tc, ts, sg4d):
    b = a4d.shape[0]
    im_blk0 = DFT_N1 // K1_TILE
    return pl.pallas_call(
        _dft2_body,
        grid=(b, DFT_N1 // K1_TILE),
        in_specs=[
            pl.BlockSpec((1, K1_TILE, DFT_N2, D_MODEL), lambda bi, kc: (bi, kc, 0, 0)),
            pl.BlockSpec((1, K1_TILE, DFT_N2, D_MODEL), lambda bi, kc: (bi, im_blk0 + kc, 0, 0)),
            pl.BlockSpec((K1_TILE, DFT_N2, DFT_N2), lambda bi, kc: (kc, 0, 0)),
            pl.BlockSpec((K1_TILE, DFT_N2, DFT_N2), lambda bi, kc: (kc, 0, 0)),
            pl.BlockSpec((1, DFT_N2, K1_TILE, D_MODEL), lambda bi, kc: (bi, 0, kc, 0)),
        ],
        out_specs=pl.BlockSpec((1, K1_TILE, DFT_N2, D_MODEL), lambda bi, kc: (bi, kc, 0, 0)),
        out_shape=jax.ShapeDtypeStruct((b, DFT_N1, DFT_N2, D_MODEL), BF16),
        compiler_params=_params("arbitrary", "arbitrary"),
        name="dft_stage2_gate",
    )(a4d, a4d, tc, ts, sg4d)


def _final_body(a_ref, w_ref, x_ref, gm_ref, fg_ref, o_ref):
    a = a_ref[0].reshape(K1_TILE * DFT_N2, D_MODEL)
    y = jnp.dot(a, w_ref[...], preferred_element_type=F32)
    gm = gm_ref[0]
    fg = fg_ref[...]
    for k in range(K1_TILE):
        x2 = x_ref[0, :, k, :] + gm * y[k * DFT_N2:(k + 1) * DFT_N2]
        ms = jnp.mean(x2 * x2, axis=-1, keepdims=True)
        o_ref[0, :, k, :] = x2 * lax.rsqrt(ms + EPS) * fg


def _fourier_outproj_final(fg4d, w_bf16, x4d, gate_mod, final_g):
    b = x4d.shape[0]
    return pl.pallas_call(
        _final_body,
        grid=(b, DFT_N1 // K1_TILE),
        in_specs=[
            pl.BlockSpec((1, K1_TILE, DFT_N2, D_MODEL), lambda bi, kc: (bi, kc, 0, 0)),
            pl.BlockSpec(w_bf16.shape, lambda bi, kc: (0, 0)),
            pl.BlockSpec((1, DFT_N2, K1_TILE, D_MODEL), lambda bi, kc: (bi, 0, kc, 0)),
            pl.BlockSpec((1, 1, D_MODEL), lambda bi, kc: (bi, 0, 0)),
            pl.BlockSpec((1, D_MODEL), lambda bi, kc: (0, 0)),
        ],
        out_specs=pl.BlockSpec((1, DFT_N2, K1_TILE, D_MODEL), lambda bi, kc: (bi, 0, kc, 0)),
        out_shape=jax.ShapeDtypeStruct(x4d.shape, F32),
        compiler_params=_params("arbitrary", "arbitrary"),
        name="fourier_outproj_final",
    )(fg4d, w_bf16, x4d, gate_mod, final_g)


def _rope_tables():
    rows = SEQ // GRID_W
    row_ids = jnp.repeat(jnp.arange(rows), GRID_W).astype(F32)
    col_ids = jnp.tile(jnp.arange(GRID_W), rows).astype(F32)
    inv_freq = ROPE_THETA ** (-jnp.arange(0, ROPE_AXIS_DIM, 2, dtype=F32) / ROPE_AXIS_DIM)
    ang = jnp.concatenate([row_ids[:, None] * inv_freq[None, :],
                           col_ids[:, None] * inv_freq[None, :]], axis=-1)
    cos, sin = jnp.cos(ang), jnp.sin(ang)
    return jnp.concatenate([cos, cos], axis=-1), jnp.concatenate([-sin, sin], axis=-1)


def _dft_tables():
    w = np.arange(FOURIER_GROUP_W)
    ang_w = 2.0 * np.pi * (np.outer(w, w) % FOURIER_GROUP_W) / FOURIER_GROUP_W
    cs = np.concatenate([np.cos(ang_w), -np.sin(ang_w)], axis=1) * 2.0 ** -4

    n1 = np.arange(DFT_N1)
    ang1 = 2.0 * np.pi * (np.outer(n1, n1) % DFT_N1) / DFT_N1
    c1, s1 = np.cos(ang1), np.sin(ang1)
    m1 = np.block([[c1, s1], [-s1, c1]]) * 2.0 ** -3

    k = n1[:, None] + DFT_N1 * np.arange(DFT_N2)[None, :]
    n2 = np.arange(DFT_N2)
    ang2 = 2.0 * np.pi * ((k[:, :, None] * n2[None, None, :]) % SEQ) / SEQ
    tc = np.cos(ang2) * 2.0 ** -3.5
    ts = np.sin(ang2) * 2.0 ** -3.5
    as_bf16 = lambda a: jnp.asarray(a, dtype=F32).astype(BF16)
    return as_bf16(cs), as_bf16(m1), as_bf16(tc), as_bf16(ts)


def kernel(x, c, norm_g, ada_w, ada_b, attn_w_in, attn_q_gain, attn_k_gain, attn_w_out,
           fourier_w_in, fourier_w_out, final_g):
    b, s, d = x.shape
    t = b * s

    c_pad = jnp.zeros((8, d), F32).at[:b].set(c)
    mod = _adaln(c_pad, ada_w, ada_b)[:, :b]
    shift = mod[:, :, None, :d]
    scale = mod[:, :, None, d:2 * d]
    gate = mod[:, :, None, 2 * d:]

    perm = np.concatenate([np.arange(0, HEAD_DIM, 2), np.arange(1, HEAD_DIM, 2)])
    n_qk_heads = N_HEADS + N_KV_HEADS
    qk_cols = (np.arange(n_qk_heads)[:, None] * HEAD_DIM + perm[None, :]).reshape(-1)
    cols = np.concatenate([qk_cols, np.arange(n_qk_heads * HEAD_DIM, ATTN_IN_WIDTH)])
    w_in0 = attn_w_in[0][:, cols].astype(BF16)
    q_gain = attn_q_gain[0][perm] * (1.0 / math.sqrt(HEAD_DIM))
    k_gain = attn_k_gain[0][perm]
    gain_vec = jnp.concatenate([jnp.tile(q_gain, N_HEADS), jnp.tile(k_gain, N_KV_HEADS),
                                jnp.ones((ATTN_IN_WIDTH - n_qk_heads * HEAD_DIM,), F32)])[None, :]
    cos_full, sin_signed = _rope_tables()

    x2d = x.reshape(t, d)
    proj = _attn_inproj(x2d, norm_g[0][None, :], scale[0], shift[0], w_in0, gain_vec,
                        cos_full, sin_signed)
    og = _attention(proj.reshape(b, s, ATTN_IN_WIDTH))
    x1 = _outproj_residual(og.reshape(t, ATTN_WIDTH), attn_w_out[0].astype(BF16), x2d, gate[0])

    cs, m1, tc, ts = _dft_tables()
    z, sg = _fourier_inproj(x1, norm_g[1][None, :], scale[1], shift[1],
                            fourier_w_in[0].astype(BF16), cs)
    a = _dft_stage1(m1, z.reshape(b, 2 * DFT_N1, DFT_N2 * d))
    fgate = _dft_stage2_gate(a.reshape(b, 2 * DFT_N1, DFT_N2, d), tc, ts,
                             sg.reshape(b, DFT_N2, DFT_N1, d))
    out = _fourier_outproj_final(fgate, fourier_w_out[0].astype(BF16),
                                 x1.reshape(b, DFT_N2, DFT_N1, d), gate[1], final_g[None, :])
    return out.reshape(b, s, d)
```

```python
import math

import numpy as np
import jax
import jax.numpy as jnp
from jax import lax
from jax.experimental import pallas as pl
from jax.experimental.pallas import tpu as pltpu

D_MODEL = 2048
BATCH = 4
SEQ = 8192
GRID_W = 64
HEAD_DIM = 128
N_HEADS = 16
N_KV_HEADS = 4
GQA_GROUP = N_HEADS // N_KV_HEADS
ATTN_WIDTH = N_HEADS * HEAD_DIM
KV_WIDTH = N_KV_HEADS * HEAD_DIM
ATTN_IN_WIDTH = 2 * ATTN_WIDTH + 2 * KV_WIDTH
ROPE_AXIS_DIM = HEAD_DIM // 2
ROPE_THETA = 10000.0
FOURIER_GROUPS = 8
FOURIER_GROUP_W = D_MODEL // FOURIER_GROUPS
EPS = 1e-6

DFT_N1 = 128
DFT_N2 = SEQ // DFT_N1

F32 = jnp.float32
BF16 = jnp.bfloat16

VMEM_LIMIT_BYTES = 56 * 1024 * 1024

ROW_TILE = 512
COL_TILE = 512
ADALN_COL_TILE = 1024
Q_TILE = 128
KV_TILE = 256
DFT1_COL_TILE = 4096
K1_TILE = 8


def _params(*semantics):
    return pltpu.CompilerParams(dimension_semantics=semantics,
                                vmem_limit_bytes=VMEM_LIMIT_BYTES)


def _silu(x):
    return x * jax.nn.sigmoid(x)


def _adaln_body(c_ref, w_ref, b_ref, o_ref):
    c_act = _silu(c_ref[...])
    o_ref[0] = jnp.dot(c_act, w_ref[0], preferred_element_type=F32,
                       precision=lax.Precision.HIGHEST) + b_ref[0]


def _adaln(c_pad, ada_w, ada_b):
    depth, d, n = ada_w.shape
    rows = c_pad.shape[0]
    return pl.pallas_call(
        _adaln_body,
        grid=(depth, n // ADALN_COL_TILE),
        in_specs=[
            pl.BlockSpec((rows, d), lambda l, j: (0, 0)),
            pl.BlockSpec((1, d, ADALN_COL_TILE), lambda l, j: (l, 0, j)),
            pl.BlockSpec((1, 1, ADALN_COL_TILE), lambda l, j: (l, 0, j)),
        ],
        out_specs=pl.BlockSpec((1, rows, ADALN_COL_TILE), lambda l, j: (l, 0, j)),
        out_shape=jax.ShapeDtypeStruct((depth, rows, n), F32),
        compiler_params=_params("arbitrary", "arbitrary"),
        name="adaln",
    )(c_pad, ada_w, ada_b.reshape(depth, 1, n))


def _modulated_norm(x_ref, g_ref, sc_ref, sh_ref):
    x = x_ref[...]
    ms = jnp.mean(x * x, axis=-1, keepdims=True)
    a = g_ref[...] * (1.0 + sc_ref[0])
    return (x * lax.rsqrt(ms + EPS) * a + sh_ref[0]).astype(BF16)


def _attn_inproj_body(x_ref, g_ref, sc_ref, sh_ref, w_ref, gain_ref, cos_ref, sin_ref,
                      o_ref, h_scr):
    j = pl.program_id(1)

    @pl.when(j == 0)
    def _():
        h_scr[...] = _modulated_norm(x_ref, g_ref, sc_ref, sh_ref)

    acc = jnp.dot(h_scr[...], w_ref[...], preferred_element_type=F32)
    n_qk_tiles = (ATTN_WIDTH + KV_WIDTH) // COL_TILE

    @pl.when(j < n_qk_tiles)
    def _():
        cos = cos_ref[...]
        sin = sin_ref[...]
        for hh in range(COL_TILE // HEAD_DIM):
            sl = slice(hh * HEAD_DIM, (hh + 1) * HEAD_DIM)
            xh = acc[:, sl]
            ms = jnp.mean(xh * xh, axis=-1, keepdims=True)
            xn = xh * lax.rsqrt(ms + EPS) * gain_ref[:, sl]
            rot = xn * cos + pltpu.roll(xn, HEAD_DIM // 2, axis=1) * sin
            o_ref[:, sl] = rot.astype(BF16)

    @pl.when(j == n_qk_tiles)
    def _():
        o_ref[...] = acc.astype(BF16)

    @pl.when(j > n_qk_tiles)
    def _():
        o_ref[...] = _silu(acc).astype(BF16)


def _attn_inproj(x2d, norm_g, scale, shift, w_bf16, gain_vec, cos_full, sin_signed):
    t, d = x2d.shape
    n = w_bf16.shape[1]
    tiles_per_batch = SEQ // ROW_TILE
    return pl.pallas_call(
        _attn_inproj_body,
        grid=(t // ROW_TILE, n // COL_TILE),
        in_specs=[
            pl.BlockSpec((ROW_TILE, d), lambda i, j: (i, 0)),
            pl.BlockSpec((1, d), lambda i, j: (0, 0)),
            pl.BlockSpec((1, 1, d), lambda i, j: (i // tiles_per_batch, 0, 0)),
            pl.BlockSpec((1, 1, d), lambda i, j: (i // tiles_per_batch, 0, 0)),
            pl.BlockSpec((d, COL_TILE), lambda i, j: (0, j)),
            pl.BlockSpec((1, COL_TILE), lambda i, j: (0, j)),
            pl.BlockSpec((ROW_TILE, HEAD_DIM), lambda i, j: (i % tiles_per_batch, 0)),
            pl.BlockSpec((ROW_TILE, HEAD_DIM), lambda i, j: (i % tiles_per_batch, 0)),
        ],
        out_specs=pl.BlockSpec((ROW_TILE, COL_TILE), lambda i, j: (i, j)),
        out_shape=jax.ShapeDtypeStruct((t, n), BF16),
        scratch_shapes=[pltpu.VMEM((ROW_TILE, d), BF16)],
        compiler_params=_params("arbitrary", "arbitrary"),
        name="attn_inproj",
    )(x2d, norm_g, scale, shift, w_bf16, gain_vec, cos_full, sin_signed)


def _attn_body(q_ref, k_ref, v_ref, sg_ref, o_ref, vt_scr, qt_scr, m_scr, l_scr, acc_scr,
               st_scr):
    n_kv_steps = SEQ // KV_TILE

    @pl.when(pl.program_id(2) == 0)
    def _():
        def fill(c, carry):
            start = pl.multiple_of(c * KV_TILE, KV_TILE)
            vt_scr[:, pl.ds(start, KV_TILE)] = v_ref[0, pl.ds(start, KV_TILE), :].T
            return carry
        lax.fori_loop(0, n_kv_steps, fill, 0)

    for hh in range(GQA_GROUP):
        qt_scr[:, hh * Q_TILE:(hh + 1) * Q_TILE] = (
            q_ref[0, :, hh * HEAD_DIM:(hh + 1) * HEAD_DIM].T)
    m_scr[...] = jnp.full(m_scr.shape, -jnp.inf, F32)
    l_scr[...] = jnp.zeros(l_scr.shape, F32)
    acc_scr[...] = jnp.zeros(acc_scr.shape, F32)

    def scores(c):
        start = pl.multiple_of(c * KV_TILE, KV_TILE)
        return jnp.dot(k_ref[0, pl.ds(start, KV_TILE), :], qt_scr[...],
                       preferred_element_type=F32)

    def consume(st, c):
        start = pl.multiple_of(c * KV_TILE, KV_TILE)
        m_old = m_scr[...]
        m_new = jnp.maximum(m_old, jnp.max(st, axis=0, keepdims=True))
        alpha = jnp.exp2(m_old - m_new)
        p = jnp.exp2(st - m_new)
        l_scr[...] = alpha * l_scr[...] + jnp.sum(p, axis=0, keepdims=True)
        pv = jnp.dot(vt_scr[:, pl.ds(start, KV_TILE)], p.astype(BF16),
                     preferred_element_type=F32)
        acc_scr[...] = alpha * acc_scr[...] + pv
        m_scr[...] = m_new

    st_scr[...] = scores(0)

    def step(i, carry):
        a = 2 * i
        st_b = scores(a + 1)
        consume(st_scr[...], a)
        st_scr[...] = scores(jnp.minimum(a + 2, n_kv_steps - 1))
        consume(st_b, a + 1)
        return carry

    lax.fori_loop(0, n_kv_steps // 2, step, 0)
    ot = acc_scr[...] / l_scr[...]
    for hh in range(GQA_GROUP):
        sl = slice(hh * HEAD_DIM, (hh + 1) * HEAD_DIM)
        o = ot[:, hh * Q_TILE:(hh + 1) * Q_TILE].T
        o_ref[0, :, sl] = (o * sg_ref[0, :, sl].astype(F32)).astype(BF16)


def _attention(proj3d):
    b, s, _ = proj3d.shape
    group_w = GQA_GROUP * HEAD_DIM
    nq = GQA_GROUP * Q_TILE
    k_blk0 = ATTN_WIDTH // HEAD_DIM
    v_blk0 = (ATTN_WIDTH + KV_WIDTH) // HEAD_DIM
    g_blk0 = (ATTN_WIDTH + 2 * KV_WIDTH) // group_w
    return pl.pallas_call(
        _attn_body,
        grid=(b, N_KV_HEADS, s // Q_TILE),
        in_specs=[
            pl.BlockSpec((1, Q_TILE, group_w), lambda bi, kh, qi: (bi, qi, kh)),
            pl.BlockSpec((1, s, HEAD_DIM), lambda bi, kh, qi: (bi, 0, k_blk0 + kh)),
            pl.BlockSpec((1, s, HEAD_DIM), lambda bi, kh, qi: (bi, 0, v_blk0 + kh)),
            pl.BlockSpec((1, Q_TILE, group_w), lambda bi, kh, qi: (bi, qi, g_blk0 + kh)),
        ],
        out_specs=pl.BlockSpec((1, Q_TILE, group_w), lambda bi, kh, qi: (bi, qi, kh)),
        out_shape=jax.ShapeDtypeStruct((b, s, ATTN_WIDTH), BF16),
        scratch_shapes=[
            pltpu.VMEM((HEAD_DIM, s), BF16),
            pltpu.VMEM((HEAD_DIM, nq), BF16),
            pltpu.VMEM((1, nq), F32),
            pltpu.VMEM((1, nq), F32),
            pltpu.VMEM((HEAD_DIM, nq), F32),
            pltpu.VMEM((KV_TILE, nq), F32),
        ],
        compiler_params=_params("arbitrary", "arbitrary", "arbitrary"),
        name="attention",
    )(proj3d, proj3d, proj3d, proj3d)


def _outproj_body(a_ref, w_ref, x_ref, gm_ref, o_ref):
    y = jnp.dot(a_ref[...], w_ref[...], preferred_element_type=F32)
    o_ref[...] = x_ref[...] + gm_ref[0] * y


def _outproj_residual(a2d, w_bf16, x2d, gate_mod):
    t, d = x2d.shape
    tiles_per_batch = SEQ // ROW_TILE
    return pl.pallas_call(
        _outproj_body,
        grid=(t // ROW_TILE,),
        in_specs=[
            pl.BlockSpec((ROW_TILE, a2d.shape[1]), lambda i: (i, 0)),
            pl.BlockSpec(w_bf16.shape, lambda i: (0, 0)),
            pl.BlockSpec((ROW_TILE, d), lambda i: (i, 0)),
            pl.BlockSpec((1, 1, d), lambda i: (i // tiles_per_batch, 0, 0)),
        ],
        out_specs=pl.BlockSpec((ROW_TILE, d), lambda i: (i, 0)),
        out_shape=jax.ShapeDtypeStruct((t, d), F32),
        compiler_params=_params("arbitrary"),
        name="attn_outproj",
    )(a2d, w_bf16, x2d, gate_mod)


def _fourier_inproj_body(x_ref, g_ref, sc_ref, sh_ref, w_ref, cs_ref, z_ref, sg_ref, h_scr):
    j = pl.program_id(1)
    n_u_tiles = D_MODEL // COL_TILE

    @pl.when(j == 0)
    def _():
        h_scr[...] = _modulated_norm(x_ref, g_ref, sc_ref, sh_ref)

    acc = jnp.dot(h_scr[...], w_ref[...], preferred_element_type=F32)

    @pl.when(j < n_u_tiles)
    def _():
        for gg in range(COL_TILE // FOURIER_GROUP_W):
            sl = slice(gg * FOURIER_GROUP_W, (gg + 1) * FOURIER_GROUP_W)
            z = jnp.dot(acc[:, sl].astype(BF16), cs_ref[...], preferred_element_type=F32)
            z_ref[0, 0, :, sl] = z[:, :FOURIER_GROUP_W].astype(BF16)
            z_ref[0, 1, :, sl] = z[:, FOURIER_GROUP_W:].astype(BF16)

    @pl.when(j >= n_u_tiles)
    def _():
        sg_ref[0] = _silu(acc)


def _fourier_inproj(x2d, norm_g, scale, shift, w_bf16, cs):
    t, d = x2d.shape
    n = w_bf16.shape[1]
    tiles_per_batch = SEQ // ROW_TILE
    n_u_tiles = D_MODEL // COL_TILE
    return pl.pallas_call(
        _fourier_inproj_body,
        grid=(t // ROW_TILE, n // COL_TILE),
        in_specs=[
            pl.BlockSpec((ROW_TILE, d), lambda i, j: (i, 0)),
            pl.BlockSpec((1, d), lambda i, j: (0, 0)),
            pl.BlockSpec((1, 1, d), lambda i, j: (i // tiles_per_batch, 0, 0)),
            pl.BlockSpec((1, 1, d), lambda i, j: (i // tiles_per_batch, 0, 0)),
            pl.BlockSpec((d, COL_TILE), lambda i, j: (0, j)),
            pl.BlockSpec(cs.shape, lambda i, j: (0, 0)),
        ],
        out_specs=[
            pl.BlockSpec((1, 2, ROW_TILE, COL_TILE),
                         lambda i, j: (i // tiles_per_batch, 0, i % tiles_per_batch,
                                       jnp.minimum(j, n_u_tiles - 1))),
            pl.BlockSpec((1, ROW_TILE, COL_TILE),
                         lambda i, j: (i // tiles_per_batch, i % tiles_per_batch,
                                       jnp.maximum(j - n_u_tiles, 0))),
        ],
        out_shape=[jax.ShapeDtypeStruct((BATCH, 2, SEQ, D_MODEL), BF16),
                   jax.ShapeDtypeStruct((BATCH, SEQ, D_MODEL), F32)],
        scratch_shapes=[pltpu.VMEM((ROW_TILE, d), BF16)],
        compiler_params=_params("arbitrary", "arbitrary"),
        name="fourier_inproj",
    )(x2d, norm_g, scale, shift, w_bf16, cs)


def _dft1_body(m_ref, x_ref, o_ref):
    o_ref[0] = jnp.dot(m_ref[...], x_ref[0], preferred_element_type=F32).astype(BF16)


def _dft_stage1(m1, z3d):
    b, rows, cols = z3d.shape
    return pl.pallas_call(
        _dft1_body,
        grid=(b, cols // DFT1_COL_TILE),
        in_specs=[
            pl.BlockSpec(m1.shape, lambda bi, j: (0, 0)),
            pl.BlockSpec((1, rows, DFT1_COL_TILE), lambda bi, j: (bi, 0, j)),
        ],
        out_specs=pl.BlockSpec((1, rows, DFT1_COL_TILE), lambda bi, j: (bi, 0, j)),
        out_shape=jax.ShapeDtypeStruct((b, rows, cols), BF16),
        compiler_params=_params("arbitrary", "arbitrary"),
        name="dft_stage1",
    )(m1, z3d)


def _dft2_body(ar_ref, ai_ref, tc_ref, ts_ref, sg_ref, o_ref):
    for k in range(K1_TILE):
        f = (jnp.dot(tc_ref[k], ar_ref[0, k], preferred_element_type=F32)
             + jnp.dot(ts_ref[k], ai_ref[0, k], preferred_element_type=F32))
        o_ref[0, k] = (f * sg_ref[0, :, k, :]).astype(BF16)


def _dft_stage2_gate(a4d, tc, ts, sg4d):
    b = a4d.shape[0]
    im_blk0 = DFT_N1 // K1_TILE
    return pl.pallas_call(
        _dft2_body,
        grid=(b, DFT_N1 // K1_TILE),
        in_specs=[
            pl.BlockSpec((1, K1_TILE, DFT_N2, D_MODEL), lambda bi, kc: (bi, kc, 0, 0)),
            pl.BlockSpec((1, K1_TILE, DFT_N2, D_MODEL), lambda bi, kc: (bi, im_blk0 + kc, 0, 0)),
            pl.BlockSpec((K1_TILE, DFT_N2, DFT_N2), lambda bi, kc: (kc, 0, 0)),
            pl.BlockSpec((K1_TILE, DFT_N2, DFT_N2), lambda bi, kc: (kc, 0, 0)),
            pl.BlockSpec((1, DFT_N2, K1_TILE, D_MODEL), lambda bi, kc: (bi, 0, kc, 0)),
        ],
        out_specs=pl.BlockSpec((1, K1_TILE, DFT_N2, D_MODEL), lambda bi, kc: (bi, kc, 0, 0)),
        out_shape=jax.ShapeDtypeStruct((b, DFT_N1, DFT_N2, D_MODEL), BF16),
        compiler_params=_params("arbitrary", "arbitrary"),
        name="dft_stage2_gate",
    )(a4d, a4d, tc, ts, sg4d)


def _final_body(a_ref, w_ref, x_ref, gm_ref, fg_ref, o_ref):
    a = a_ref[0].reshape(K1_TILE * DFT_N2, D_MODEL)
    y = jnp.dot(a, w_ref[...], preferred_element_type=F32)
    gm = gm_ref[0]
    fg = fg_ref[...]
    for k in range(K1_TILE):
        x2 = x_ref[0, :, k, :] + gm * y[k * DFT_N2:(k + 1) * DFT_N2]
        ms = jnp.mean(x2 * x2, axis=-1, keepdims=True)
        o_ref[0, :, k, :] = x2 * lax.rsqrt(ms + EPS) * fg


def _fourier_outproj_final(fg4d, w_bf16, x4d, gate_mod, final_g):
    b = x4d.shape[0]
    return pl.pallas_call(
        _final_body,
        grid=(b, DFT_N1 // K1_TILE),
        in_specs=[
            pl.BlockSpec((1, K1_TILE, DFT_N2, D_MODEL), lambda bi, kc: (bi, kc, 0, 0)),
            pl.BlockSpec(w_bf16.shape, lambda bi, kc: (0, 0)),
            pl.BlockSpec((1, DFT_N2, K1_TILE, D_MODEL), lambda bi, kc: (bi, 0, kc, 0)),
            pl.BlockSpec((1, 1, D_MODEL), lambda bi, kc: (bi, 0, 0)),
            pl.BlockSpec((1, D_MODEL), lambda bi, kc: (0, 0)),
        ],
        out_specs=pl.BlockSpec((1, DFT_N2, K1_TILE, D_MODEL), lambda bi, kc: (bi, 0, kc, 0)),
        out_shape=jax.ShapeDtypeStruct(x4d.shape, F32),
        compiler_params=_params("arbitrary", "arbitrary"),
        name="fourier_outproj_final",
    )(fg4d, w_bf16, x4d, gate_mod, final_g)


def _rope_tables():
    rows = SEQ // GRID_W
    row_ids = jnp.repeat(jnp.arange(rows), GRID_W).astype(F32)
    col_ids = jnp.tile(jnp.arange(GRID_W), rows).astype(F32)
    inv_freq = ROPE_THETA ** (-jnp.arange(0, ROPE_AXIS_DIM, 2, dtype=F32) / ROPE_AXIS_DIM)
    ang = jnp.concatenate([row_ids[:, None] * inv_freq[None, :],
                           col_ids[:, None] * inv_freq[None, :]], axis=-1)
    cos, sin = jnp.cos(ang), jnp.sin(ang)
    return jnp.concatenate([cos, cos], axis=-1), jnp.concatenate([-sin, sin], axis=-1)


def _dft_tables():
    w = np.arange(FOURIER_GROUP_W)
    ang_w = 2.0 * np.pi * (np.outer(w, w) % FOURIER_GROUP_W) / FOURIER_GROUP_W
    cs = np.concatenate([np.cos(ang_w), -np.sin(ang_w)], axis=1) * 2.0 ** -4

    n1 = np.arange(DFT_N1)
    ang1 = 2.0 * np.pi * (np.outer(n1, n1) % DFT_N1) / DFT_N1
    c1, s1 = np.cos(ang1), np.sin(ang1)
    m1 = np.block([[c1, s1], [-s1, c1]]) * 2.0 ** -3

    k = n1[:, None] + DFT_N1 * np.arange(DFT_N2)[None, :]
    n2 = np.arange(DFT_N2)
    ang2 = 2.0 * np.pi * ((k[:, :, None] * n2[None, None, :]) % SEQ) / SEQ
    tc = np.cos(ang2) * 2.0 ** -3.5
    ts = np.sin(ang2) * 2.0 ** -3.5
    as_bf16 = lambda a: jnp.asarray(a, dtype=F32).astype(BF16)
    return as_bf16(cs), as_bf16(m1), as_bf16(tc), as_bf16(ts)


def kernel(x, c, norm_g, ada_w, ada_b, attn_w_in, attn_q_gain, attn_k_gain, attn_w_out,
           fourier_w_in, fourier_w_out, final_g):
    b, s, d = x.shape
    t = b * s

    c_pad = jnp.zeros((8, d), F32).at[:b].set(c)
    mod = _adaln(c_pad, ada_w, ada_b)[:, :b]
    shift = mod[:, :, None, :d]
    scale = mod[:, :, None, d:2 * d]
    gate = mod[:, :, None, 2 * d:]

    perm = np.concatenate([np.arange(0, HEAD_DIM, 2), np.arange(1, HEAD_DIM, 2)])
    n_qk_heads = N_HEADS + N_KV_HEADS
    qk_cols = (np.arange(n_qk_heads)[:, None] * HEAD_DIM + perm[None, :]).reshape(-1)
    cols = np.concatenate([qk_cols, np.arange(n_qk_heads * HEAD_DIM, ATTN_IN_WIDTH)])
    w_in0 = attn_w_in[0][:, cols].astype(BF16)
    q_gain = attn_q_gain[0][perm] * (math.log2(math.e) / math.sqrt(HEAD_DIM))
    k_gain = attn_k_gain[0][perm]
    gain_vec = jnp.concatenate([jnp.tile(q_gain, N_HEADS), jnp.tile(k_gain, N_KV_HEADS),
                                jnp.ones((ATTN_IN_WIDTH - n_qk_heads * HEAD_DIM,), F32)])[None, :]
    cos_full, sin_signed = _rope_tables()

    x2d = x.reshape(t, d)
    proj = _attn_inproj(x2d, norm_g[0][None, :], scale[0], shift[0], w_in0, gain_vec,
                        cos_full, sin_signed)
    og = _attention(proj.reshape(b, s, ATTN_IN_WIDTH))
    x1 = _outproj_residual(og.reshape(t, ATTN_WIDTH), attn_w_out[0].astype(BF16), x2d, gate[0])

    cs, m1, tc, ts = _dft_tables()
    z, sg = _fourier_inproj(x1, norm_g[1][None, :], scale[1], shift[1],
                            fourier_w_in[0].astype(BF16), cs)
    a = _dft_stage1(m1, z.reshape(b, 2 * DFT_N1, DFT_N2 * d))
    fgate = _dft_stage2_gate(a.reshape(b, 2 * DFT_N1, DFT_N2, d), tc, ts,
                             sg.reshape(b, DFT_N2, DFT_N1, d))
    out = _fourier_outproj_final(fgate, fourier_w_out[0].astype(BF16),
                                 x1.reshape(b, DFT_N2, DFT_N1, d), gate[1], final_g[None, :])
    return out.reshape(b, s, d)
```

```python
import math

import numpy as np
import jax
import jax.numpy as jnp
from jax import lax
from jax.experimental import pallas as pl
from jax.experimental.pallas import tpu as pltpu

D_MODEL = 2048
BATCH = 4
SEQ = 8192
GRID_W = 64
HEAD_DIM = 128
N_HEADS = 16
N_KV_HEADS = 4
GQA_GROUP = N_HEADS // N_KV_HEADS
ATTN_WIDTH = N_HEADS * HEAD_DIM
KV_WIDTH = N_KV_HEADS * HEAD_DIM
ATTN_IN_WIDTH = 2 * ATTN_WIDTH + 2 * KV_WIDTH
ROPE_AXIS_DIM = HEAD_DIM // 2
ROPE_THETA = 10000.0
FOURIER_GROUPS = 8
FOURIER_GROUP_W = D_MODEL // FOURIER_GROUPS
EPS = 1e-6

DFT_N1 = 128
DFT_N2 = SEQ // DFT_N1

F32 = jnp.float32
BF16 = jnp.bfloat16

SCORE_BOUND_LOG2 = 64.0
BF16_NORM_SLACK = 1.01

VMEM_LIMIT_BYTES = 56 * 1024 * 1024

ROW_TILE = 512
COL_TILE = 512
ADALN_COL_TILE = 1024
Q_TILE = 128
KV_TILE = 256
CHUNKS_PER_TRIP = 8
DFT1_COL_TILE = 4096
K1_TILE = 8


def _params(*semantics):
    return pltpu.CompilerParams(dimension_semantics=semantics,
                                vmem_limit_bytes=VMEM_LIMIT_BYTES)


def _silu(x):
    return x * jax.nn.sigmoid(x)


def _adaln_body(c_ref, w_ref, b_ref, o_ref):
    c_act = _silu(c_ref[...])
    o_ref[0] = jnp.dot(c_act, w_ref[0], preferred_element_type=F32,
                       precision=lax.Precision.HIGHEST) + b_ref[0]


def _adaln(c_pad, ada_w, ada_b):
    depth, d, n = ada_w.shape
    rows = c_pad.shape[0]
    return pl.pallas_call(
        _adaln_body,
        grid=(depth, n // ADALN_COL_TILE),
        in_specs=[
            pl.BlockSpec((rows, d), lambda l, j: (0, 0)),
            pl.BlockSpec((1, d, ADALN_COL_TILE), lambda l, j: (l, 0, j)),
            pl.BlockSpec((1, 1, ADALN_COL_TILE), lambda l, j: (l, 0, j)),
        ],
        out_specs=pl.BlockSpec((1, rows, ADALN_COL_TILE), lambda l, j: (l, 0, j)),
        out_shape=jax.ShapeDtypeStruct((depth, rows, n), F32),
        compiler_params=_params("arbitrary", "arbitrary"),
        name="adaln",
    )(c_pad, ada_w, ada_b.reshape(depth, 1, n))


def _modulated_norm(x_ref, g_ref, sc_ref, sh_ref):
    x = x_ref[...]
    ms = jnp.mean(x * x, axis=-1, keepdims=True)
    a = g_ref[...] * (1.0 + sc_ref[0])
    return (x * lax.rsqrt(ms + EPS) * a + sh_ref[0]).astype(BF16)


def _attn_inproj_body(x_ref, g_ref, sc_ref, sh_ref, w_ref, gain_ref, cos_ref, sin_ref,
                      o_ref, h_scr):
    j = pl.program_id(1)

    @pl.when(j == 0)
    def _():
        h_scr[...] = _modulated_norm(x_ref, g_ref, sc_ref, sh_ref)

    acc = jnp.dot(h_scr[...], w_ref[...], preferred_element_type=F32)
    n_qk_tiles = (ATTN_WIDTH + KV_WIDTH) // COL_TILE

    @pl.when(j < n_qk_tiles)
    def _():
        cos = cos_ref[...]
        sin = sin_ref[...]
        for hh in range(COL_TILE // HEAD_DIM):
            sl = slice(hh * HEAD_DIM, (hh + 1) * HEAD_DIM)
            xh = acc[:, sl]
            ms = jnp.mean(xh * xh, axis=-1, keepdims=True)
            xn = xh * lax.rsqrt(ms + EPS) * gain_ref[:, sl]
            rot = xn * cos + pltpu.roll(xn, HEAD_DIM // 2, axis=1) * sin
            o_ref[:, sl] = rot.astype(BF16)

    @pl.when(j == n_qk_tiles)
    def _():
        o_ref[...] = acc.astype(BF16)

    @pl.when(j > n_qk_tiles)
    def _():
        o_ref[...] = _silu(acc).astype(BF16)


def _attn_inproj(x2d, norm_g, scale, shift, w_bf16, gain_vec, cos_full, sin_signed):
    t, d = x2d.shape
    n = w_bf16.shape[1]
    tiles_per_batch = SEQ // ROW_TILE
    return pl.pallas_call(
        _attn_inproj_body,
        grid=(t // ROW_TILE, n // COL_TILE),
        in_specs=[
            pl.BlockSpec((ROW_TILE, d), lambda i, j: (i, 0)),
            pl.BlockSpec((1, d), lambda i, j: (0, 0)),
            pl.BlockSpec((1, 1, d), lambda i, j: (i // tiles_per_batch, 0, 0)),
            pl.BlockSpec((1, 1, d), lambda i, j: (i // tiles_per_batch, 0, 0)),
            pl.BlockSpec((d, COL_TILE), lambda i, j: (0, j)),
            pl.BlockSpec((1, COL_TILE), lambda i, j: (0, j)),
            pl.BlockSpec((ROW_TILE, HEAD_DIM), lambda i, j: (i % tiles_per_batch, 0)),
            pl.BlockSpec((ROW_TILE, HEAD_DIM), lambda i, j: (i % tiles_per_batch, 0)),
        ],
        out_specs=pl.BlockSpec((ROW_TILE, COL_TILE), lambda i, j: (i, j)),
        out_shape=jax.ShapeDtypeStruct((t, n), BF16),
        scratch_shapes=[pltpu.VMEM((ROW_TILE, d), BF16)],
        compiler_params=_params("arbitrary", "arbitrary"),
        name="attn_inproj",
    )(x2d, norm_g, scale, shift, w_bf16, gain_vec, cos_full, sin_signed)


def _attn_body(bounded_ref, q_ref, k_ref, v_ref, sg_ref, o_ref, vt_scr, qt_scr, m_scr, l_scr,
               acc_scr, st_scr):
    n_kv_steps = SEQ // KV_TILE

    @pl.when(pl.program_id(2) == 0)
    def _():
        def fill(c, carry):
            start = pl.multiple_of(c * KV_TILE, KV_TILE)
            vt_scr[:, pl.ds(start, KV_TILE)] = v_ref[0, pl.ds(start, KV_TILE), :].T
            return carry
        lax.fori_loop(0, n_kv_steps, fill, 0)

    for hh in range(GQA_GROUP):
        qt_scr[:, hh * Q_TILE:(hh + 1) * Q_TILE] = (
            q_ref[0, :, hh * HEAD_DIM:(hh + 1) * HEAD_DIM].T)
    m_scr[...] = jnp.full(m_scr.shape, -jnp.inf, F32)
    l_scr[...] = jnp.zeros(l_scr.shape, F32)
    acc_scr[...] = jnp.zeros(acc_scr.shape, F32)

    def scores(c):
        start = pl.multiple_of(c * KV_TILE, KV_TILE)
        return jnp.dot(k_ref[0, pl.ds(start, KV_TILE), :], qt_scr[...],
                       preferred_element_type=F32)

    def weighted_values(p, c):
        start = pl.multiple_of(c * KV_TILE, KV_TILE)
        return jnp.dot(vt_scr[:, pl.ds(start, KV_TILE)], p.astype(BF16),
                       preferred_element_type=F32)

    def consume_online(st, c):
        m_old = m_scr[...]
        m_new = jnp.maximum(m_old, jnp.max(st, axis=0, keepdims=True))
        alpha = jnp.exp2(m_old - m_new)
        p = jnp.exp2(st - m_new)
        l_scr[0:1] = alpha * l_scr[0:1] + jnp.sum(p, axis=0, keepdims=True)
        acc_scr[...] = alpha * acc_scr[...] + weighted_values(p, c)
        m_scr[...] = m_new

    def consume_bounded(st, c):
        p = jnp.exp2(st)
        l_scr[...] += jnp.sum(p.reshape(KV_TILE // 8, 8, p.shape[1]), axis=0)
        acc_scr[...] += weighted_values(p, c)

    def run(consume):
        st_scr[...] = scores(0)

        def step(i, carry):
            first = i * CHUNKS_PER_TRIP
            st = st_scr[...]
            for j in range(CHUNKS_PER_TRIP):
                nxt = jnp.minimum(first + j + 1, n_kv_steps - 1)
                st_next = scores(nxt)
                consume(st, first + j)
                st = st_next
            st_scr[...] = st
            return carry

        lax.fori_loop(0, n_kv_steps // CHUNKS_PER_TRIP, step, 0)

    @pl.when(bounded_ref[0] != 0)
    def _():
        run(consume_bounded)

    @pl.when(bounded_ref[0] == 0)
    def _():
        run(consume_online)

    ot = acc_scr[...] / jnp.sum(l_scr[...], axis=0, keepdims=True)
    for hh in range(GQA_GROUP):
        sl = slice(hh * HEAD_DIM, (hh + 1) * HEAD_DIM)
        o = ot[:, hh * Q_TILE:(hh + 1) * Q_TILE].T
        o_ref[0, :, sl] = (o * sg_ref[0, :, sl].astype(F32)).astype(BF16)


def _attention(bounded, proj3d):
    b, s, _ = proj3d.shape
    group_w = GQA_GROUP * HEAD_DIM
    nq = GQA_GROUP * Q_TILE
    k_blk0 = ATTN_WIDTH // HEAD_DIM
    v_blk0 = (ATTN_WIDTH + KV_WIDTH) // HEAD_DIM
    g_blk0 = (ATTN_WIDTH + 2 * KV_WIDTH) // group_w
    return pl.pallas_call(
        _attn_body,
        grid=(b, N_KV_HEADS, s // Q_TILE),
        in_specs=[
            pl.BlockSpec(memory_space=pltpu.SMEM),
            pl.BlockSpec((1, Q_TILE, group_w), lambda bi, kh, qi: (bi, qi, kh)),
            pl.BlockSpec((1, s, HEAD_DIM), lambda bi, kh, qi: (bi, 0, k_blk0 + kh)),
            pl.BlockSpec((1, s, HEAD_DIM), lambda bi, kh, qi: (bi, 0, v_blk0 + kh)),
            pl.BlockSpec((1, Q_TILE, group_w), lambda bi, kh, qi: (bi, qi, g_blk0 + kh)),
        ],
        out_specs=pl.BlockSpec((1, Q_TILE, group_w), lambda bi, kh, qi: (bi, qi, kh)),
        out_shape=jax.ShapeDtypeStruct((b, s, ATTN_WIDTH), BF16),
        scratch_shapes=[
            pltpu.VMEM((HEAD_DIM, s), BF16),
            pltpu.VMEM((HEAD_DIM, nq), BF16),
            pltpu.VMEM((1, nq), F32),
            pltpu.VMEM((8, nq), F32),
            pltpu.VMEM((HEAD_DIM, nq), F32),
            pltpu.VMEM((KV_TILE, nq), F32),
        ],
        compiler_params=_params("arbitrary", "arbitrary", "arbitrary"),
        name="attention",
    )(bounded, proj3d, proj3d, proj3d, proj3d)


def _outproj_body(a_ref, w_ref, x_ref, gm_ref, o_ref):
    y = jnp.dot(a_ref[...], w_ref[...], preferred_element_type=F32)
    o_ref[...] = x_ref[...] + gm_ref[0] * y


def _outproj_residual(a2d, w_bf16, x2d, gate_mod):
    t, d = x2d.shape
    tiles_per_batch = SEQ // ROW_TILE
    return pl.pallas_call(
        _outproj_body,
        grid=(t // ROW_TILE,),
        in_specs=[
            pl.BlockSpec((ROW_TILE, a2d.shape[1]), lambda i: (i, 0)),
            pl.BlockSpec(w_bf16.shape, lambda i: (0, 0)),
            pl.BlockSpec((ROW_TILE, d), lambda i: (i, 0)),
            pl.BlockSpec((1, 1, d), lambda i: (i // tiles_per_batch, 0, 0)),
        ],
        out_specs=pl.BlockSpec((ROW_TILE, d), lambda i: (i, 0)),
        out_shape=jax.ShapeDtypeStruct((t, d), F32),
        compiler_params=_params("arbitrary"),
        name="attn_outproj",
    )(a2d, w_bf16, x2d, gate_mod)


def _fourier_inproj_body(x_ref, g_ref, sc_ref, sh_ref, w_ref, cs_ref, z_ref, sg_ref, h_scr):
    j = pl.program_id(1)
    n_u_tiles = D_MODEL // COL_TILE

    @pl.when(j == 0)
    def _():
        h_scr[...] = _modulated_norm(x_ref, g_ref, sc_ref, sh_ref)

    acc = jnp.dot(h_scr[...], w_ref[...], preferred_element_type=F32)

    @pl.when(j < n_u_tiles)
    def _():
        for gg in range(COL_TILE // FOURIER_GROUP_W):
            sl = slice(gg * FOURIER_GROUP_W, (gg + 1) * FOURIER_GROUP_W)
            z = jnp.dot(acc[:, sl].astype(BF16), cs_ref[...], preferred_element_type=F32)
            z_ref[0, 0, :, sl] = z[:, :FOURIER_GROUP_W].astype(BF16)
            z_ref[0, 1, :, sl] = z[:, FOURIER_GROUP_W:].astype(BF16)

    @pl.when(j >= n_u_tiles)
    def _():
        sg_ref[0] = _silu(acc)


def _fourier_inproj(x2d, norm_g, scale, shift, w_bf16, cs):
    t, d = x2d.shape
    n = w_bf16.shape[1]
    tiles_per_batch = SEQ // ROW_TILE
    n_u_tiles = D_MODEL // COL_TILE
    return pl.pallas_call(
        _fourier_inproj_body,
        grid=(t // ROW_TILE, n // COL_TILE),
        in_specs=[
            pl.BlockSpec((ROW_TILE, d), lambda i, j: (i, 0)),
            pl.BlockSpec((1, d), lambda i, j: (0, 0)),
            pl.BlockSpec((1, 1, d), lambda i, j: (i // tiles_per_batch, 0, 0)),
            pl.BlockSpec((1, 1, d), lambda i, j: (i // tiles_per_batch, 0, 0)),
            pl.BlockSpec((d, COL_TILE), lambda i, j: (0, j)),
            pl.BlockSpec(cs.shape, lambda i, j: (0, 0)),
        ],
        out_specs=[
            pl.BlockSpec((1, 2, ROW_TILE, COL_TILE),
                         lambda i, j: (i // tiles_per_batch, 0, i % tiles_per_batch,
                                       jnp.minimum(j, n_u_tiles - 1))),
            pl.BlockSpec((1, ROW_TILE, COL_TILE),
                         lambda i, j: (i // tiles_per_batch, i % tiles_per_batch,
                                       jnp.maximum(j - n_u_tiles, 0))),
        ],
        out_shape=[jax.ShapeDtypeStruct((BATCH, 2, SEQ, D_MODEL), BF16),
                   jax.ShapeDtypeStruct((BATCH, SEQ, D_MODEL), F32)],
        scratch_shapes=[pltpu.VMEM((ROW_TILE, d), BF16)],
        compiler_params=_params("arbitrary", "arbitrary"),
        name="fourier_inproj",
    )(x2d, norm_g, scale, shift, w_bf16, cs)


def _dft1_body(m_ref, x_ref, o_ref):
    o_ref[0] = jnp.dot(m_ref[...], x_ref[0], preferred_element_type=F32).astype(BF16)


def _dft_stage1(m1, z3d):
    b, rows, cols = z3d.shape
    return pl.pallas_call(
        _dft1_body,
        grid=(b, cols // DFT1_COL_TILE),
        in_specs=[
            pl.BlockSpec(m1.shape, lambda bi, j: (0, 0)),
            pl.BlockSpec((1, rows, DFT1_COL_TILE), lambda bi, j: (bi, 0, j)),
        ],
        out_specs=pl.BlockSpec((1, rows, DFT1_COL_TILE), lambda bi, j: (bi, 0, j)),
        out_shape=jax.ShapeDtypeStruct((b, rows, cols), BF16),
        compiler_params=_params("arbitrary", "arbitrary"),
        name="dft_stage1",
    )(m1, z3d)


def _dft2_body(ar_ref, ai_ref, tc_ref, ts_ref, sg_ref, o_ref):
    for k in range(K1_TILE):
        f = (jnp.dot(tc_ref[k], ar_ref[0, k], preferred_element_type=F32)
             + jnp.dot(ts_ref[k], ai_ref[0, k], preferred_element_type=F32))
        o_ref[0, k] = (f * sg_ref[0, :, k, :]).astype(BF16)


def _dft_stage2_gate(a4d, tc, ts, sg4d):
    b = a4d.shape[0]
    im_blk0 = DFT_N1 // K1_TILE
    return pl.pallas_call(
        _dft2_body,
        grid=(b, DFT_N1 // K1_TILE),
        in_specs=[
            pl.BlockSpec((1, K1_TILE, DFT_N2, D_MODEL), lambda bi, kc: (bi, kc, 0, 0)),
            pl.BlockSpec((1, K1_TILE, DFT_N2, D_MODEL), lambda bi, kc: (bi, im_blk0 + kc, 0, 0)),
            pl.BlockSpec((K1_TILE, DFT_N2, DFT_N2), lambda bi, kc: (kc, 0, 0)),
            pl.BlockSpec((K1_TILE, DFT_N2, DFT_N2), lambda bi, kc: (kc, 0, 0)),
            pl.BlockSpec((1, DFT_N2, K1_TILE, D_MODEL), lambda bi, kc: (bi, 0, kc, 0)),
        ],
        out_specs=pl.BlockSpec((1, K1_TILE, DFT_N2, D_MODEL), lambda bi, kc: (bi, kc, 0, 0)),
        out_shape=jax.ShapeDtypeStruct((b, DFT_N1, DFT_N2, D_MODEL), BF16),
        compiler_params=_params("arbitrary", "arbitrary"),
        name="dft_stage2_gate",
    )(a4d, a4d, tc, ts, sg4d)


def _final_body(a_ref, w_ref, x_ref, gm_ref, fg_ref, o_ref):
    a = a_ref[0].reshape(K1_TILE * DFT_N2, D_MODEL)
    y = jnp.dot(a, w_ref[...], preferred_element_type=F32)
    gm = gm_ref[0]
    fg = fg_ref[...]
    for k in range(K1_TILE):
        x2 = x_ref[0, :, k, :] + gm * y[k * DFT_N2:(k + 1) * DFT_N2]
        ms = jnp.mean(x2 * x2, axis=-1, keepdims=True)
        o_ref[0, :, k, :] = x2 * lax.rsqrt(ms + EPS) * fg


def _fourier_outproj_final(fg4d, w_bf16, x4d, gate_mod, final_g):
    b = x4d.shape[0]
    return pl.pallas_call(
        _final_body,
        grid=(b, DFT_N1 // K1_TILE),
        in_specs=[
            pl.BlockSpec((1, K1_TILE, DFT_N2, D_MODEL), lambda bi, kc: (bi, kc, 0, 0)),
            pl.BlockSpec(w_bf16.shape, lambda bi, kc: (0, 0)),
            pl.BlockSpec((1, DFT_N2, K1_TILE, D_MODEL), lambda bi, kc: (bi, 0, kc, 0)),
            pl.BlockSpec((1, 1, D_MODEL), lambda bi, kc: (bi, 0, 0)),
            pl.BlockSpec((1, D_MODEL), lambda bi, kc: (0, 0)),
        ],
        out_specs=pl.BlockSpec((1, DFT_N2, K1_TILE, D_MODEL), lambda bi, kc: (bi, 0, kc, 0)),
        out_shape=jax.ShapeDtypeStruct(x4d.shape, F32),
        compiler_params=_params("arbitrary", "arbitrary"),
        name="fourier_outproj_final",
    )(fg4d, w_bf16, x4d, gate_mod, final_g)


def _rope_tables():
    rows = SEQ // GRID_W
    row_ids = jnp.repeat(jnp.arange(rows), GRID_W).astype(F32)
    col_ids = jnp.tile(jnp.arange(GRID_W), rows).astype(F32)
    inv_freq = ROPE_THETA ** (-jnp.arange(0, ROPE_AXIS_DIM, 2, dtype=F32) / ROPE_AXIS_DIM)
    ang = jnp.concatenate([row_ids[:, None] * inv_freq[None, :],
                           col_ids[:, None] * inv_freq[None, :]], axis=-1)
    cos, sin = jnp.cos(ang), jnp.sin(ang)
    return jnp.concatenate([cos, cos], axis=-1), jnp.concatenate([-sin, sin], axis=-1)


def _dft_tables():
    w = np.arange(FOURIER_GROUP_W)
    ang_w = 2.0 * np.pi * (np.outer(w, w) % FOURIER_GROUP_W) / FOURIER_GROUP_W
    cs = np.concatenate([np.cos(ang_w), -np.sin(ang_w)], axis=1) * 2.0 ** -4

    n1 = np.arange(DFT_N1)
    ang1 = 2.0 * np.pi * (np.outer(n1, n1) % DFT_N1) / DFT_N1
    c1, s1 = np.cos(ang1), np.sin(ang1)
    m1 = np.block([[c1, s1], [-s1, c1]]) * 2.0 ** -3

    k = n1[:, None] + DFT_N1 * np.arange(DFT_N2)[None, :]
    n2 = np.arange(DFT_N2)
    ang2 = 2.0 * np.pi * ((k[:, :, None] * n2[None, None, :]) % SEQ) / SEQ
    tc = np.cos(ang2) * 2.0 ** -3.5
    ts = np.sin(ang2) * 2.0 ** -3.5
    as_bf16 = lambda a: jnp.asarray(a, dtype=F32).astype(BF16)
    return as_bf16(cs), as_bf16(m1), as_bf16(tc), as_bf16(ts)


def kernel(x, c, norm_g, ada_w, ada_b, attn_w_in, attn_q_gain, attn_k_gain, attn_w_out,
           fourier_w_in, fourier_w_out, final_g):
    b, s, d = x.shape
    t = b * s

    c_pad = jnp.zeros((8, d), F32).at[:b].set(c)
    mod = _adaln(c_pad, ada_w, ada_b)[:, :b]
    shift = mod[:, :, None, :d]
    scale = mod[:, :, None, d:2 * d]
    gate = mod[:, :, None, 2 * d:]

    perm = np.concatenate([np.arange(0, HEAD_DIM, 2), np.arange(1, HEAD_DIM, 2)])
    n_qk_heads = N_HEADS + N_KV_HEADS
    qk_cols = (np.arange(n_qk_heads)[:, None] * HEAD_DIM + perm[None, :]).reshape(-1)
    cols = np.concatenate([qk_cols, np.arange(n_qk_heads * HEAD_DIM, ATTN_IN_WIDTH)])
    w_in0 = attn_w_in[0][:, cols].astype(BF16)
    q_gain = attn_q_gain[0][perm] * (math.log2(math.e) / math.sqrt(HEAD_DIM))
    k_gain = attn_k_gain[0][perm]
    gain_vec = jnp.concatenate([jnp.tile(q_gain, N_HEADS), jnp.tile(k_gain, N_KV_HEADS),
                                jnp.ones((ATTN_IN_WIDTH - n_qk_heads * HEAD_DIM,), F32)])[None, :]
    cos_full, sin_signed = _rope_tables()

    x2d = x.reshape(t, d)
    proj = _attn_inproj(x2d, norm_g[0][None, :], scale[0], shift[0], w_in0, gain_vec,
                        cos_full, sin_signed)
    score_bound = (HEAD_DIM * BF16_NORM_SLACK * jnp.max(jnp.abs(q_gain)) * jnp.max(jnp.abs(k_gain)))
    bounded = (score_bound <= SCORE_BOUND_LOG2).astype(jnp.int32).reshape(1)
    og = _attention(bounded, proj.reshape(b, s, ATTN_IN_WIDTH))
    x1 = _outproj_residual(og.reshape(t, ATTN_WIDTH), attn_w_out[0].astype(BF16), x2d, gate[0])

    cs, m1, tc, ts = _dft_tables()
    z, sg = _fourier_inproj(x1, norm_g[1][None, :], scale[1], shift[1],
                            fourier_w_in[0].astype(BF16), cs)
    a = _dft_stage1(m1, z.reshape(b, 2 * DFT_N1, DFT_N2 * d))
    fgate = _dft_stage2_gate(a.reshape(b, 2 * DFT_N1, DFT_N2, d), tc, ts,
                             sg.reshape(b, DFT_N2, DFT_N1, d))
    out = _fourier_outproj_final(fgate, fourier_w_out[0].astype(BF16),
                                 x1.reshape(b, DFT_N2, DFT_N1, d), gate[1], final_g[None, :])
    return out.reshape(b, s, d)
```

```python
import math

import numpy as np
import jax
import jax.numpy as jnp
from jax import lax
from jax.experimental import pallas as pl
from jax.experimental.pallas import tpu as pltpu

D_MODEL = 2048
BATCH = 4
SEQ = 8192
GRID_W = 64
HEAD_DIM = 128
N_HEADS = 16
N_KV_HEADS = 4
GQA_GROUP = N_HEADS // N_KV_HEADS
ATTN_WIDTH = N_HEADS * HEAD_DIM
KV_WIDTH = N_KV_HEADS * HEAD_DIM
ATTN_IN_WIDTH = 2 * ATTN_WIDTH + 2 * KV_WIDTH
ROPE_AXIS_DIM = HEAD_DIM // 2
ROPE_THETA = 10000.0
FOURIER_GROUPS = 8
FOURIER_GROUP_W = D_MODEL // FOURIER_GROUPS
EPS = 1e-6

DFT_N1 = 128
DFT_N2 = SEQ // DFT_N1

F32 = jnp.float32
BF16 = jnp.bfloat16

SCORE_BOUND_LOG2 = 64.0
BF16_NORM_SLACK = 1.01

VMEM_LIMIT_BYTES = 56 * 1024 * 1024

ROW_TILE = 512
COL_TILE = 512
ADALN_COL_TILE = 1024
Q_TILE = 128
KV_TILE = 256
BOUNDED_CHUNKS_PER_TRIP = 32
ONLINE_CHUNKS_PER_TRIP = 4
DFT1_COL_TILE = 4096
K1_TILE = 8


def _params(*semantics):
    return pltpu.CompilerParams(dimension_semantics=semantics,
                                vmem_limit_bytes=VMEM_LIMIT_BYTES)


def _silu(x):
    return x * jax.nn.sigmoid(x)


def _adaln_body(c_ref, w_ref, b_ref, o_ref):
    c_act = _silu(c_ref[...])
    o_ref[0] = jnp.dot(c_act, w_ref[0], preferred_element_type=F32,
                       precision=lax.Precision.HIGHEST) + b_ref[0]


def _adaln(c_pad, ada_w, ada_b):
    depth, d, n = ada_w.shape
    rows = c_pad.shape[0]
    return pl.pallas_call(
        _adaln_body,
        grid=(depth, n // ADALN_COL_TILE),
        in_specs=[
            pl.BlockSpec((rows, d), lambda l, j: (0, 0)),
            pl.BlockSpec((1, d, ADALN_COL_TILE), lambda l, j: (l, 0, j)),
            pl.BlockSpec((1, 1, ADALN_COL_TILE), lambda l, j: (l, 0, j)),
        ],
        out_specs=pl.BlockSpec((1, rows, ADALN_COL_TILE), lambda l, j: (l, 0, j)),
        out_shape=jax.ShapeDtypeStruct((depth, rows, n), F32),
        compiler_params=_params("arbitrary", "arbitrary"),
        name="adaln",
    )(c_pad, ada_w, ada_b.reshape(depth, 1, n))


def _modulated_norm(x_ref, g_ref, sc_ref, sh_ref):
    x = x_ref[...]
    ms = jnp.mean(x * x, axis=-1, keepdims=True)
    a = g_ref[...] * (1.0 + sc_ref[0])
    return (x * lax.rsqrt(ms + EPS) * a + sh_ref[0]).astype(BF16)


def _attn_inproj_body(x_ref, g_ref, sc_ref, sh_ref, w_ref, gain_ref, cos_ref, sin_ref, o_ref):
    h = _modulated_norm(x_ref, g_ref, sc_ref, sh_ref)
    cos = cos_ref[...]
    sin = sin_ref[...]
    n_qk_tiles = (ATTN_WIDTH + KV_WIDTH) // COL_TILE
    for j in range(ATTN_IN_WIDTH // COL_TILE):
        cols = slice(j * COL_TILE, (j + 1) * COL_TILE)
        acc = jnp.dot(h, w_ref[:, cols], preferred_element_type=F32)
        if j < n_qk_tiles:
            for hh in range(COL_TILE // HEAD_DIM):
                sl = slice(j * COL_TILE + hh * HEAD_DIM, j * COL_TILE + (hh + 1) * HEAD_DIM)
                xh = acc[:, hh * HEAD_DIM:(hh + 1) * HEAD_DIM]
                ms = jnp.mean(xh * xh, axis=-1, keepdims=True)
                xn = xh * lax.rsqrt(ms + EPS) * gain_ref[:, sl]
                rot = xn * cos + pltpu.roll(xn, HEAD_DIM // 2, axis=1) * sin
                o_ref[:, sl] = rot.astype(BF16)
        elif j == n_qk_tiles:
            o_ref[:, cols] = acc.astype(BF16)
        else:
            o_ref[:, cols] = _silu(acc).astype(BF16)


def _resident(shape):
    return pl.BlockSpec(shape, lambda *_: (0,) * len(shape), pipeline_mode=pl.Buffered(1))


def _attn_inproj(x2d, norm_g, scale, shift, w_bf16, gain_vec, cos_full, sin_signed):
    t, d = x2d.shape
    n = w_bf16.shape[1]
    tiles_per_batch = SEQ // ROW_TILE
    return pl.pallas_call(
        _attn_inproj_body,
        grid=(t // ROW_TILE,),
        in_specs=[
            pl.BlockSpec((ROW_TILE, d), lambda i: (i, 0)),
            _resident((1, d)),
            pl.BlockSpec((1, 1, d), lambda i: (i // tiles_per_batch, 0, 0)),
            pl.BlockSpec((1, 1, d), lambda i: (i // tiles_per_batch, 0, 0)),
            _resident((d, n)),
            _resident((1, n)),
            pl.BlockSpec((ROW_TILE, HEAD_DIM), lambda i: (i % tiles_per_batch, 0)),
            pl.BlockSpec((ROW_TILE, HEAD_DIM), lambda i: (i % tiles_per_batch, 0)),
        ],
        out_specs=pl.BlockSpec((ROW_TILE, n), lambda i: (i, 0)),
        out_shape=jax.ShapeDtypeStruct((t, n), BF16),
        compiler_params=_params("arbitrary"),
        name="attn_inproj",
    )(x2d, norm_g, scale, shift, w_bf16, gain_vec, cos_full, sin_signed)


def _attn_body(bounded_ref, q_ref, k_ref, v_ref, sg_ref, o_ref, vt_scr, qt_scr, m_scr, l_scr,
               acc_scr, st_scr):
    n_kv_steps = SEQ // KV_TILE

    @pl.when(pl.program_id(2) == 0)
    def _():
        def fill(c, carry):
            start = pl.multiple_of(c * KV_TILE, KV_TILE)
            vt_scr[:, pl.ds(start, KV_TILE)] = v_ref[0, pl.ds(start, KV_TILE), :].T
            return carry
        lax.fori_loop(0, n_kv_steps, fill, 0)

    for hh in range(GQA_GROUP):
        qt_scr[:, hh * Q_TILE:(hh + 1) * Q_TILE] = (
            q_ref[0, :, hh * HEAD_DIM:(hh + 1) * HEAD_DIM].T)
    m_scr[...] = jnp.full(m_scr.shape, -jnp.inf, F32)
    l_scr[...] = jnp.zeros(l_scr.shape, F32)
    acc_scr[...] = jnp.zeros(acc_scr.shape, F32)

    def scores(c):
        start = pl.multiple_of(c * KV_TILE, KV_TILE)
        return jnp.dot(k_ref[0, pl.ds(start, KV_TILE), :], qt_scr[...],
                       preferred_element_type=F32)

    def weighted_values(p, c):
        start = pl.multiple_of(c * KV_TILE, KV_TILE)
        return jnp.dot(vt_scr[:, pl.ds(start, KV_TILE)], p.astype(BF16),
                       preferred_element_type=F32)

    def consume_online(st, c):
        m_old = m_scr[...]
        m_new = jnp.maximum(m_old, jnp.max(st, axis=0, keepdims=True))
        alpha = jnp.exp2(m_old - m_new)
        p = jnp.exp2(st - m_new)
        l_scr[0:1] = alpha * l_scr[0:1] + jnp.sum(p, axis=0, keepdims=True)
        acc_scr[...] = alpha * acc_scr[...] + weighted_values(p, c)
        m_scr[...] = m_new

    def consume_bounded(st, c):
        p = jnp.exp2(st)
        l_scr[...] += jnp.sum(p.reshape(KV_TILE // 8, 8, p.shape[1]), axis=0)
        acc_scr[...] += weighted_values(p, c)

    def run(consume, chunks_per_trip):
        st_scr[...] = scores(0)

        def step(i, carry):
            first = i * chunks_per_trip
            st = st_scr[...]
            for j in range(chunks_per_trip):
                nxt = jnp.minimum(first + j + 1, n_kv_steps - 1)
                st_next = scores(nxt)
                consume(st, first + j)
                st = st_next
            st_scr[...] = st
            return carry

        lax.fori_loop(0, n_kv_steps // chunks_per_trip, step, 0)

    @pl.when(bounded_ref[0] != 0)
    def _():
        run(consume_bounded, BOUNDED_CHUNKS_PER_TRIP)

    @pl.when(bounded_ref[0] == 0)
    def _():
        run(consume_online, ONLINE_CHUNKS_PER_TRIP)

    ot = acc_scr[...] / jnp.sum(l_scr[...], axis=0, keepdims=True)
    for hh in range(GQA_GROUP):
        sl = slice(hh * HEAD_DIM, (hh + 1) * HEAD_DIM)
        o = ot[:, hh * Q_TILE:(hh + 1) * Q_TILE].T
        o_ref[0, :, sl] = (o * sg_ref[0, :, sl].astype(F32)).astype(BF16)


def _attention(bounded, proj3d):
    b, s, _ = proj3d.shape
    group_w = GQA_GROUP * HEAD_DIM
    nq = GQA_GROUP * Q_TILE
    k_blk0 = ATTN_WIDTH // HEAD_DIM
    v_blk0 = (ATTN_WIDTH + KV_WIDTH) // HEAD_DIM
    g_blk0 = (ATTN_WIDTH + 2 * KV_WIDTH) // group_w
    return pl.pallas_call(
        _attn_body,
        grid=(b, N_KV_HEADS, s // Q_TILE),
        in_specs=[
            pl.BlockSpec(memory_space=pltpu.SMEM),
            pl.BlockSpec((1, Q_TILE, group_w), lambda bi, kh, qi: (bi, qi, kh)),
            pl.BlockSpec((1, s, HEAD_DIM), lambda bi, kh, qi: (bi, 0, k_blk0 + kh)),
            pl.BlockSpec((1, s, HEAD_DIM), lambda bi, kh, qi: (bi, 0, v_blk0 + kh)),
            pl.BlockSpec((1, Q_TILE, group_w), lambda bi, kh, qi: (bi, qi, g_blk0 + kh)),
        ],
        out_specs=pl.BlockSpec((1, Q_TILE, group_w), lambda bi, kh, qi: (bi, qi, kh)),
        out_shape=jax.ShapeDtypeStruct((b, s, ATTN_WIDTH), BF16),
        scratch_shapes=[
            pltpu.VMEM((HEAD_DIM, s), BF16),
            pltpu.VMEM((HEAD_DIM, nq), BF16),
            pltpu.VMEM((1, nq), F32),
            pltpu.VMEM((8, nq), F32),
            pltpu.VMEM((HEAD_DIM, nq), F32),
            pltpu.VMEM((KV_TILE, nq), F32),
        ],
        compiler_params=_params("arbitrary", "arbitrary", "arbitrary"),
        name="attention",
    )(bounded, proj3d, proj3d, proj3d, proj3d)


def _outproj_body(a_ref, w_ref, x_ref, gm_ref, o_ref):
    y = jnp.dot(a_ref[...], w_ref[...], preferred_element_type=F32)
    o_ref[...] = x_ref[...] + gm_ref[0] * y


def _outproj_residual(a2d, w_bf16, x2d, gate_mod):
    t, d = x2d.shape
    tiles_per_batch = SEQ // ROW_TILE
    return pl.pallas_call(
        _outproj_body,
        grid=(t // ROW_TILE,),
        in_specs=[
            pl.BlockSpec((ROW_TILE, a2d.shape[1]), lambda i: (i, 0)),
            pl.BlockSpec(w_bf16.shape, lambda i: (0, 0)),
            pl.BlockSpec((ROW_TILE, d), lambda i: (i, 0)),
            pl.BlockSpec((1, 1, d), lambda i: (i // tiles_per_batch, 0, 0)),
        ],
        out_specs=pl.BlockSpec((ROW_TILE, d), lambda i: (i, 0)),
        out_shape=jax.ShapeDtypeStruct((t, d), F32),
        compiler_params=_params("arbitrary"),
        name="attn_outproj",
    )(a2d, w_bf16, x2d, gate_mod)


def _fourier_inproj_body(x_ref, g_ref, sc_ref, sh_ref, w_ref, cs_ref, z_ref, sg_ref):
    h = _modulated_norm(x_ref, g_ref, sc_ref, sh_ref)
    n_u_tiles = D_MODEL // COL_TILE
    for j in range(2 * n_u_tiles):
        acc = jnp.dot(h, w_ref[:, j * COL_TILE:(j + 1) * COL_TILE], preferred_element_type=F32)
        if j < n_u_tiles:
            for gg in range(COL_TILE // FOURIER_GROUP_W):
                sl = slice(j * COL_TILE + gg * FOURIER_GROUP_W,
                           j * COL_TILE + (gg + 1) * FOURIER_GROUP_W)
                u = acc[:, gg * FOURIER_GROUP_W:(gg + 1) * FOURIER_GROUP_W].astype(BF16)
                z = jnp.dot(u, cs_ref[...], preferred_element_type=F32)
                z_ref[0, 0, :, sl] = z[:, :FOURIER_GROUP_W].astype(BF16)
                z_ref[0, 1, :, sl] = z[:, FOURIER_GROUP_W:].astype(BF16)
        else:
            cols = slice((j - n_u_tiles) * COL_TILE, (j - n_u_tiles + 1) * COL_TILE)
            sg_ref[0, :, cols] = _silu(acc)


def _fourier_inproj(x2d, norm_g, scale, shift, w_bf16, cs):
    t, d = x2d.shape
    tiles_per_batch = SEQ // ROW_TILE
    return pl.pallas_call(
        _fourier_inproj_body,
        grid=(t // ROW_TILE,),
        in_specs=[
            pl.BlockSpec((ROW_TILE, d), lambda i: (i, 0)),
            _resident((1, d)),
            pl.BlockSpec((1, 1, d), lambda i: (i // tiles_per_batch, 0, 0)),
            pl.BlockSpec((1, 1, d), lambda i: (i // tiles_per_batch, 0, 0)),
            _resident(w_bf16.shape),
            _resident(cs.shape),
        ],
        out_specs=[
            pl.BlockSpec((1, 2, ROW_TILE, D_MODEL),
                         lambda i: (i // tiles_per_batch, 0, i % tiles_per_batch, 0)),
            pl.BlockSpec((1, ROW_TILE, D_MODEL),
                         lambda i: (i // tiles_per_batch, i % tiles_per_batch, 0)),
        ],
        out_shape=[jax.ShapeDtypeStruct((BATCH, 2, SEQ, D_MODEL), BF16),
                   jax.ShapeDtypeStruct((BATCH, SEQ, D_MODEL), F32)],
        compiler_params=_params("arbitrary"),
        name="fourier_inproj",
    )(x2d, norm_g, scale, shift, w_bf16, cs)


def _dft1_body(m_ref, x_ref, o_ref):
    o_ref[0] = jnp.dot(m_ref[...], x_ref[0], preferred_element_type=F32).astype(BF16)


def _dft_stage1(m1, z3d):
    b, rows, cols = z3d.shape
    return pl.pallas_call(
        _dft1_body,
        grid=(b, cols // DFT1_COL_TILE),
        in_specs=[
            pl.BlockSpec(m1.shape, lambda bi, j: (0, 0)),
            pl.BlockSpec((1, rows, DFT1_COL_TILE), lambda bi, j: (bi, 0, j)),
        ],
        out_specs=pl.BlockSpec((1, rows, DFT1_COL_TILE), lambda bi, j: (bi, 0, j)),
        out_shape=jax.ShapeDtypeStruct((b, rows, cols), BF16),
        compiler_params=_params("arbitrary", "arbitrary"),
        name="dft_stage1",
    )(m1, z3d)


def _dft2_body(ar_ref, ai_ref, tc_ref, ts_ref, sg_ref, o_ref):
    for k in range(K1_TILE):
        f = (jnp.dot(tc_ref[k], ar_ref[0, k], preferred_element_type=F32)
             + jnp.dot(ts_ref[k], ai_ref[0, k], preferred_element_type=F32))
        o_ref[0, k] = (f * sg_ref[0, :, k, :]).astype(BF16)


def _dft_stage2_gate(a4d, tc, ts, sg4d):
    b = a4d.shape[0]
    im_blk0 = DFT_N1 // K1_TILE
    return pl.pallas_call(
        _dft2_body,
        grid=(b, DFT_N1 // K1_TILE),
        in_specs=[
            pl.BlockSpec((1, K1_TILE, DFT_N2, D_MODEL), lambda bi, kc: (bi, kc, 0, 0)),
            pl.BlockSpec((1, K1_TILE, DFT_N2, D_MODEL), lambda bi, kc: (bi, im_blk0 + kc, 0, 0)),
            pl.BlockSpec((K1_TILE, DFT_N2, DFT_N2), lambda bi, kc: (kc, 0, 0)),
            pl.BlockSpec((K1_TILE, DFT_N2, DFT_N2), lambda bi, kc: (kc, 0, 0)),
            pl.BlockSpec((1, DFT_N2, K1_TILE, D_MODEL), lambda bi, kc: (bi, 0, kc, 0)),
        ],
        out_specs=pl.BlockSpec((1, K1_TILE, DFT_N2, D_MODEL), lambda bi, kc: (bi, kc, 0, 0)),
        out_shape=jax.ShapeDtypeStruct((b, DFT_N1, DFT_N2, D_MODEL), BF16),
        compiler_params=_params("arbitrary", "arbitrary"),
        name="dft_stage2_gate",
    )(a4d, a4d, tc, ts, sg4d)


def _final_body(a_ref, w_ref, x_ref, gm_ref, fg_ref, o_ref):
    a = a_ref[0].reshape(K1_TILE * DFT_N2, D_MODEL)
    y = jnp.dot(a, w_ref[...], preferred_element_type=F32)
    gm = gm_ref[0]
    fg = fg_ref[...]
    for k in range(K1_TILE):
        x2 = x_ref[0, :, k, :] + gm * y[k * DFT_N2:(k + 1) * DFT_N2]
        ms = jnp.mean(x2 * x2, axis=-1, keepdims=True)
        o_ref[0, :, k, :] = x2 * lax.rsqrt(ms + EPS) * fg


def _fourier_outproj_final(fg4d, w_bf16, x4d, gate_mod, final_g):
    b = x4d.shape[0]
    return pl.pallas_call(
        _final_body,
        grid=(b, DFT_N1 // K1_TILE),
        in_specs=[
            pl.BlockSpec((1, K1_TILE, DFT_N2, D_MODEL), lambda bi, kc: (bi, kc, 0, 0)),
            pl.BlockSpec(w_bf16.shape, lambda bi, kc: (0, 0)),
            pl.BlockSpec((1, DFT_N2, K1_TILE, D_MODEL), lambda bi, kc: (bi, 0, kc, 0)),
            pl.BlockSpec((1, 1, D_MODEL), lambda bi, kc: (bi, 0, 0)),
            pl.BlockSpec((1, D_MODEL), lambda bi, kc: (0, 0)),
        ],
        out_specs=pl.BlockSpec((1, DFT_N2, K1_TILE, D_MODEL), lambda bi, kc: (bi, 0, kc, 0)),
        out_shape=jax.ShapeDtypeStruct(x4d.shape, F32),
        compiler_params=_params("arbitrary", "arbitrary"),
        name="fourier_outproj_final",
    )(fg4d, w_bf16, x4d, gate_mod, final_g)


def _rope_tables():
    rows = SEQ // GRID_W
    row_ids = jnp.repeat(jnp.arange(rows), GRID_W).astype(F32)
    col_ids = jnp.tile(jnp.arange(GRID_W), rows).astype(F32)
    inv_freq = ROPE_THETA ** (-jnp.arange(0, ROPE_AXIS_DIM, 2, dtype=F32) / ROPE_AXIS_DIM)
    ang = jnp.concatenate([row_ids[:, None] * inv_freq[None, :],
                           col_ids[:, None] * inv_freq[None, :]], axis=-1)
    cos, sin = jnp.cos(ang), jnp.sin(ang)
    return jnp.concatenate([cos, cos], axis=-1), jnp.concatenate([-sin, sin], axis=-1)


def _dft_tables():
    w = np.arange(FOURIER_GROUP_W)
    ang_w = 2.0 * np.pi * (np.outer(w, w) % FOURIER_GROUP_W) / FOURIER_GROUP_W
    cs = np.concatenate([np.cos(ang_w), -np.sin(ang_w)], axis=1) * 2.0 ** -4

    n1 = np.arange(DFT_N1)
    ang1 = 2.0 * np.pi * (np.outer(n1, n1) % DFT_N1) / DFT_N1
    c1, s1 = np.cos(ang1), np.sin(ang1)
    m1 = np.block([[c1, s1], [-s1, c1]]) * 2.0 ** -3

    k = n1[:, None] + DFT_N1 * np.arange(DFT_N2)[None, :]
    n2 = np.arange(DFT_N2)
    ang2 = 2.0 * np.pi * ((k[:, :, None] * n2[None, None, :]) % SEQ) / SEQ
    tc = np.cos(ang2) * 2.0 ** -3.5
    ts = np.sin(ang2) * 2.0 ** -3.5
    as_bf16 = lambda a: jnp.asarray(a, dtype=F32).astype(BF16)
    return as_bf16(cs), as_bf16(m1), as_bf16(tc), as_bf16(ts)


def kernel(x, c, norm_g, ada_w, ada_b, attn_w_in, attn_q_gain, attn_k_gain, attn_w_out,
           fourier_w_in, fourier_w_out, final_g):
    b, s, d = x.shape
    t = b * s

    c_pad = jnp.zeros((8, d), F32).at[:b].set(c)
    mod = _adaln(c_pad, ada_w, ada_b)[:, :b]
    shift = mod[:, :, None, :d]
    scale = mod[:, :, None, d:2 * d]
    gate = mod[:, :, None, 2 * d:]

    perm = np.concatenate([np.arange(0, HEAD_DIM, 2), np.arange(1, HEAD_DIM, 2)])
    n_qk_heads = N_HEADS + N_KV_HEADS
    qk_cols = (np.arange(n_qk_heads)[:, None] * HEAD_DIM + perm[None, :]).reshape(-1)
    cols = np.concatenate([qk_cols, np.arange(n_qk_heads * HEAD_DIM, ATTN_IN_WIDTH)])
    w_in0 = attn_w_in[0][:, cols].astype(BF16)
    q_gain = attn_q_gain[0][perm] * (math.log2(math.e) / math.sqrt(HEAD_DIM))
    k_gain = attn_k_gain[0][perm]
    gain_vec = jnp.concatenate([jnp.tile(q_gain, N_HEADS), jnp.tile(k_gain, N_KV_HEADS),
                                jnp.ones((ATTN_IN_WIDTH - n_qk_heads * HEAD_DIM,), F32)])[None, :]
    cos_full, sin_signed = _rope_tables()

    x2d = x.reshape(t, d)
    proj = _attn_inproj(x2d, norm_g[0][None, :], scale[0], shift[0], w_in0, gain_vec,
                        cos_full, sin_signed)
    score_bound = (HEAD_DIM * BF16_NORM_SLACK * jnp.max(jnp.abs(q_gain)) * jnp.max(jnp.abs(k_gain)))
    bounded = (score_bound <= SCORE_BOUND_LOG2).astype(jnp.int32).reshape(1)
    og = _attention(bounded, proj.reshape(b, s, ATTN_IN_WIDTH))
    x1 = _outproj_residual(og.reshape(t, ATTN_WIDTH), attn_w_out[0].astype(BF16), x2d, gate[0])

    cs, m1, tc, ts = _dft_tables()
    z, sg = _fourier_inproj(x1, norm_g[1][None, :], scale[1], shift[1],
                            fourier_w_in[0].astype(BF16), cs)
    a = _dft_stage1(m1, z.reshape(b, 2 * DFT_N1, DFT_N2 * d))
    fgate = _dft_stage2_gate(a.reshape(b, 2 * DFT_N1, DFT_N2, d), tc, ts,
                             sg.reshape(b, DFT_N2, DFT_N1, d))
    out = _fourier_outproj_final(fgate, fourier_w_out[0].astype(BF16),
                                 x1.reshape(b, DFT_N2, DFT_N1, d), gate[1], final_g[None, :])
    return out.reshape(b, s, d)
```

```python
import math

import numpy as np
import jax
import jax.numpy as jnp
from jax import lax
from jax.experimental import pallas as pl
from jax.experimental.pallas import tpu as pltpu

D_MODEL = 2048
BATCH = 4
SEQ = 8192
GRID_W = 64
HEAD_DIM = 128
N_HEADS = 16
N_KV_HEADS = 4
GQA_GROUP = N_HEADS // N_KV_HEADS
ATTN_WIDTH = N_HEADS * HEAD_DIM
KV_WIDTH = N_KV_HEADS * HEAD_DIM
ATTN_IN_WIDTH = 2 * ATTN_WIDTH + 2 * KV_WIDTH
ROPE_AXIS_DIM = HEAD_DIM // 2
ROPE_THETA = 10000.0
FOURIER_GROUPS = 8
FOURIER_GROUP_W = D_MODEL // FOURIER_GROUPS
EPS = 1e-6

DFT_N1 = 128
DFT_N2 = SEQ // DFT_N1

F32 = jnp.float32
BF16 = jnp.bfloat16

SCORE_BOUND_LOG2 = 64.0
BF16_NORM_SLACK = 1.01

VMEM_LIMIT_BYTES = 56 * 1024 * 1024

ROW_TILE = 512
COL_TILE = 512
ADALN_COL_TILE = 1024
Q_TILE = 256
KV_TILE = 256
BOUNDED_CHUNKS_PER_TRIP = 32
ONLINE_CHUNKS_PER_TRIP = 4
DFT_CH_TILE = 128
DFT_N2_PITCH = DFT_N2 + 8
DFT_N1_PITCH = DFT_N1 + 8
DFT_WIDEN_UNROLL = 8
DFT_STAGE1_UNROLL = 4
DFT_STAGE2_UNROLL = 8


def _params(*semantics):
    return pltpu.CompilerParams(dimension_semantics=semantics,
                                vmem_limit_bytes=VMEM_LIMIT_BYTES)


def _silu(x):
    return x * jax.nn.sigmoid(x)


def _adaln_body(c_ref, w_ref, b_ref, o_ref):
    c_act = _silu(c_ref[...])
    o_ref[0] = jnp.dot(c_act, w_ref[0], preferred_element_type=F32,
                       precision=lax.Precision.HIGHEST) + b_ref[0]


def _adaln(c_pad, ada_w, ada_b):
    depth, d, n = ada_w.shape
    rows = c_pad.shape[0]
    return pl.pallas_call(
        _adaln_body,
        grid=(depth, n // ADALN_COL_TILE),
        in_specs=[
            pl.BlockSpec((rows, d), lambda l, j: (0, 0)),
            pl.BlockSpec((1, d, ADALN_COL_TILE), lambda l, j: (l, 0, j)),
            pl.BlockSpec((1, 1, ADALN_COL_TILE), lambda l, j: (l, 0, j)),
        ],
        out_specs=pl.BlockSpec((1, rows, ADALN_COL_TILE), lambda l, j: (l, 0, j)),
        out_shape=jax.ShapeDtypeStruct((depth, rows, n), F32),
        compiler_params=_params("arbitrary", "arbitrary"),
        name="adaln",
    )(c_pad, ada_w, ada_b.reshape(depth, 1, n))


def _modulated_norm(x_ref, g_ref, sc_ref, sh_ref):
    x = x_ref[...]
    ms = jnp.mean(x * x, axis=-1, keepdims=True)
    a = g_ref[...] * (1.0 + sc_ref[0])
    return (x * lax.rsqrt(ms + EPS) * a + sh_ref[0]).astype(BF16)


def _attn_inproj_body(x_ref, g_ref, sc_ref, sh_ref, w_ref, gain_ref, cos_ref, sin_ref, o_ref):
    h = _modulated_norm(x_ref, g_ref, sc_ref, sh_ref)
    cos = cos_ref[...]
    sin = sin_ref[...]
    n_qk_tiles = (ATTN_WIDTH + KV_WIDTH) // COL_TILE
    for j in range(ATTN_IN_WIDTH // COL_TILE):
        cols = slice(j * COL_TILE, (j + 1) * COL_TILE)
        acc = jnp.dot(h, w_ref[:, cols], preferred_element_type=F32)
        if j < n_qk_tiles:
            for hh in range(COL_TILE // HEAD_DIM):
                sl = slice(j * COL_TILE + hh * HEAD_DIM, j * COL_TILE + (hh + 1) * HEAD_DIM)
                xh = acc[:, hh * HEAD_DIM:(hh + 1) * HEAD_DIM]
                ms = jnp.mean(xh * xh, axis=-1, keepdims=True)
                xn = xh * lax.rsqrt(ms + EPS) * gain_ref[:, sl]
                rot = xn * cos + pltpu.roll(xn, HEAD_DIM // 2, axis=1) * sin
                o_ref[:, sl] = rot.astype(BF16)
        elif j == n_qk_tiles:
            o_ref[:, cols] = acc.astype(BF16)
        else:
            o_ref[:, cols] = _silu(acc).astype(BF16)


def _resident(shape):
    return pl.BlockSpec(shape, lambda *_: (0,) * len(shape), pipeline_mode=pl.Buffered(1))


def _attn_inproj(x2d, norm_g, scale, shift, w_bf16, gain_vec, cos_full, sin_signed):
    t, d = x2d.shape
    n = w_bf16.shape[1]
    tiles_per_batch = SEQ // ROW_TILE
    return pl.pallas_call(
        _attn_inproj_body,
        grid=(t // ROW_TILE,),
        in_specs=[
            pl.BlockSpec((ROW_TILE, d), lambda i: (i, 0)),
            _resident((1, d)),
            pl.BlockSpec((1, 1, d), lambda i: (i // tiles_per_batch, 0, 0)),
            pl.BlockSpec((1, 1, d), lambda i: (i // tiles_per_batch, 0, 0)),
            _resident((d, n)),
            _resident((1, n)),
            pl.BlockSpec((ROW_TILE, HEAD_DIM), lambda i: (i % tiles_per_batch, 0)),
            pl.BlockSpec((ROW_TILE, HEAD_DIM), lambda i: (i % tiles_per_batch, 0)),
        ],
        out_specs=pl.BlockSpec((ROW_TILE, n), lambda i: (i, 0)),
        out_shape=jax.ShapeDtypeStruct((t, n), BF16),
        compiler_params=_params("arbitrary"),
        name="attn_inproj",
    )(x2d, norm_g, scale, shift, w_bf16, gain_vec, cos_full, sin_signed)


def _attn_body(bounded_ref, q_ref, k_ref, v_ref, sg_ref, o_ref, vt_scr, qt_scr, m_scr, l_scr,
               acc_scr, st_scr):
    n_kv_steps = SEQ // KV_TILE

    @pl.when(pl.program_id(2) == 0)
    def _():
        def fill(c, carry):
            start = pl.multiple_of(c * KV_TILE, KV_TILE)
            vt_scr[:, pl.ds(start, KV_TILE)] = v_ref[0, pl.ds(start, KV_TILE), :].T
            return carry
        lax.fori_loop(0, n_kv_steps, fill, 0)

    for hh in range(GQA_GROUP):
        qt_scr[:, hh * Q_TILE:(hh + 1) * Q_TILE] = (
            q_ref[0, :, hh * HEAD_DIM:(hh + 1) * HEAD_DIM].T)
    m_scr[...] = jnp.full(m_scr.shape, -jnp.inf, F32)
    l_scr[...] = jnp.zeros(l_scr.shape, F32)
    acc_scr[...] = jnp.zeros(acc_scr.shape, F32)

    def scores(c):
        start = pl.multiple_of(c * KV_TILE, KV_TILE)
        return jnp.dot(k_ref[0, pl.ds(start, KV_TILE), :], qt_scr[...],
                       preferred_element_type=F32)

    def weighted_values(p, c):
        start = pl.multiple_of(c * KV_TILE, KV_TILE)
        return jnp.dot(vt_scr[:, pl.ds(start, KV_TILE)], p.astype(BF16),
                       preferred_element_type=F32)

    def consume_online(st, c):
        m_old = m_scr[...]
        m_new = jnp.maximum(m_old, jnp.max(st, axis=0, keepdims=True))
        alpha = jnp.exp2(m_old - m_new)
        p = jnp.exp2(st - m_new)
        l_scr[0:1] = alpha * l_scr[0:1] + jnp.sum(p, axis=0, keepdims=True)
        acc_scr[...] = alpha * acc_scr[...] + weighted_values(p, c)
        m_scr[...] = m_new

    def consume_bounded(st, c):
        p = jnp.exp2(st)
        l_scr[...] += jnp.sum(p.reshape(KV_TILE // 8, 8, p.shape[1]), axis=0)
        acc_scr[...] += weighted_values(p, c)

    def run(consume, chunks_per_trip):
        st_scr[...] = scores(0)

        def step(i, carry):
            first = i * chunks_per_trip
            st = st_scr[...]
            for j in range(chunks_per_trip):
                nxt = jnp.minimum(first + j + 1, n_kv_steps - 1)
                st_next = scores(nxt)
                consume(st, first + j)
                st = st_next
            st_scr[...] = st
            return carry

        lax.fori_loop(0, n_kv_steps // chunks_per_trip, step, 0)

    @pl.when(bounded_ref[0] != 0)
    def _():
        run(consume_bounded, BOUNDED_CHUNKS_PER_TRIP)

    @pl.when(bounded_ref[0] == 0)
    def _():
        run(consume_online, ONLINE_CHUNKS_PER_TRIP)

    ot = acc_scr[...] / jnp.sum(l_scr[...], axis=0, keepdims=True)
    for hh in range(GQA_GROUP):
        sl = slice(hh * HEAD_DIM, (hh + 1) * HEAD_DIM)
        o = ot[:, hh * Q_TILE:(hh + 1) * Q_TILE].T
        o_ref[0, :, sl] = (o * sg_ref[0, :, sl].astype(F32)).astype(BF16)


def _attention(bounded, proj3d):
    b, s, _ = proj3d.shape
    group_w = GQA_GROUP * HEAD_DIM
    nq = GQA_GROUP * Q_TILE
    k_blk0 = ATTN_WIDTH // HEAD_DIM
    v_blk0 = (ATTN_WIDTH + KV_WIDTH) // HEAD_DIM
    g_blk0 = (ATTN_WIDTH + 2 * KV_WIDTH) // group_w
    return pl.pallas_call(
        _attn_body,
        grid=(b, N_KV_HEADS, s // Q_TILE),
        in_specs=[
            pl.BlockSpec(memory_space=pltpu.SMEM),
            pl.BlockSpec((1, Q_TILE, group_w), lambda bi, kh, qi: (bi, qi, kh)),
            pl.BlockSpec((1, s, HEAD_DIM), lambda bi, kh, qi: (bi, 0, k_blk0 + kh)),
            pl.BlockSpec((1, s, HEAD_DIM), lambda bi, kh, qi: (bi, 0, v_blk0 + kh)),
            pl.BlockSpec((1, Q_TILE, group_w), lambda bi, kh, qi: (bi, qi, g_blk0 + kh)),
        ],
        out_specs=pl.BlockSpec((1, Q_TILE, group_w), lambda bi, kh, qi: (bi, qi, kh)),
        out_shape=jax.ShapeDtypeStruct((b, s, ATTN_WIDTH), BF16),
        scratch_shapes=[
            pltpu.VMEM((HEAD_DIM, s), BF16),
            pltpu.VMEM((HEAD_DIM, nq), BF16),
            pltpu.VMEM((1, nq), F32),
            pltpu.VMEM((8, nq), F32),
            pltpu.VMEM((HEAD_DIM, nq), F32),
            pltpu.VMEM((KV_TILE, nq), F32),
        ],
        compiler_params=_params("arbitrary", "arbitrary", "arbitrary"),
        name="attention",
    )(bounded, proj3d, proj3d, proj3d, proj3d)


def _outproj_body(a_ref, w_ref, x_ref, gm_ref, o_ref):
    y = jnp.dot(a_ref[...], w_ref[...], preferred_element_type=F32)
    o_ref[...] = x_ref[...] + gm_ref[0] * y


def _outproj_residual(a2d, w_bf16, x2d, gate_mod):
    t, d = x2d.shape
    tiles_per_batch = SEQ // ROW_TILE
    return pl.pallas_call(
        _outproj_body,
        grid=(t // ROW_TILE,),
        in_specs=[
            pl.BlockSpec((ROW_TILE, a2d.shape[1]), lambda i: (i, 0)),
            pl.BlockSpec(w_bf16.shape, lambda i: (0, 0)),
            pl.BlockSpec((ROW_TILE, d), lambda i: (i, 0)),
            pl.BlockSpec((1, 1, d), lambda i: (i // tiles_per_batch, 0, 0)),
        ],
        out_specs=pl.BlockSpec((ROW_TILE, d), lambda i: (i, 0)),
        out_shape=jax.ShapeDtypeStruct((t, d), F32),
        compiler_params=_params("arbitrary"),
        name="attn_outproj",
    )(a2d, w_bf16, x2d, gate_mod)


def _fourier_inproj_body(x_ref, g_ref, sc_ref, sh_ref, w_ref, cs_ref, z_ref, sg_ref):
    h = _modulated_norm(x_ref, g_ref, sc_ref, sh_ref)
    n_u_tiles = D_MODEL // COL_TILE
    for j in range(2 * n_u_tiles):
        acc = jnp.dot(h, w_ref[:, j * COL_TILE:(j + 1) * COL_TILE], preferred_element_type=F32)
        if j < n_u_tiles:
            for gg in range(COL_TILE // FOURIER_GROUP_W):
                sl = slice(j * COL_TILE + gg * FOURIER_GROUP_W,
                           j * COL_TILE + (gg + 1) * FOURIER_GROUP_W)
                u = acc[:, gg * FOURIER_GROUP_W:(gg + 1) * FOURIER_GROUP_W].astype(BF16)
                z = jnp.dot(u, cs_ref[...], preferred_element_type=F32)
                z_ref[0, 0, :, sl] = z[:, :FOURIER_GROUP_W].astype(BF16)
                z_ref[0, 1, :, sl] = z[:, FOURIER_GROUP_W:].astype(BF16)
        else:
            cols = slice((j - n_u_tiles) * COL_TILE, (j - n_u_tiles + 1) * COL_TILE)
            sg_ref[0, :, cols] = _silu(acc).astype(BF16)


def _fourier_inproj(x2d, norm_g, scale, shift, w_bf16, cs):
    t, d = x2d.shape
    tiles_per_batch = SEQ // ROW_TILE
    return pl.pallas_call(
        _fourier_inproj_body,
        grid=(t // ROW_TILE,),
        in_specs=[
            pl.BlockSpec((ROW_TILE, d), lambda i: (i, 0)),
            _resident((1, d)),
            pl.BlockSpec((1, 1, d), lambda i: (i // tiles_per_batch, 0, 0)),
            pl.BlockSpec((1, 1, d), lambda i: (i // tiles_per_batch, 0, 0)),
            _resident(w_bf16.shape),
            _resident(cs.shape),
        ],
        out_specs=[
            pl.BlockSpec((1, 2, ROW_TILE, D_MODEL),
                         lambda i: (i // tiles_per_batch, 0, i % tiles_per_batch, 0)),
            pl.BlockSpec((1, ROW_TILE, D_MODEL),
                         lambda i: (i // tiles_per_batch, i % tiles_per_batch, 0)),
        ],
        out_shape=[jax.ShapeDtypeStruct((BATCH, 2, SEQ, D_MODEL), BF16),
                   jax.ShapeDtypeStruct((BATCH, SEQ, D_MODEL), BF16)],
        compiler_params=_params("arbitrary"),
        name="fourier_inproj",
    )(x2d, norm_g, scale, shift, w_bf16, cs)


def _seq_dft_body(z_ref, sg_ref, m_ref, t_ref, o_ref, zf_scr, a_scr, f_scr):
    c = DFT_CH_TILE
    im_rows = DFT_N1 * DFT_N2_PITCH

    def widen(i, carry):
        for u in range(DFT_WIDEN_UNROLL):
            n1 = i * DFT_WIDEN_UNROLL + u
            src = pl.multiple_of(n1 * DFT_N2, DFT_N2)
            dst = pl.multiple_of(n1 * DFT_N2_PITCH, 8)
            zf_scr[pl.ds(dst, DFT_N2), :] = z_ref[0, 0, pl.ds(src, DFT_N2), :].astype(F32)
            zf_scr[pl.ds(im_rows + dst, DFT_N2), :] = (
                z_ref[0, 1, pl.ds(src, DFT_N2), :].astype(F32))
        return carry

    lax.fori_loop(0, DFT_N1 // DFT_WIDEN_UNROLL, widen, 0)

    def stage1(i, carry):
        for u in range(DFT_STAGE1_UNROLL):
            n2 = (i * DFT_STAGE1_UNROLL + u) * 2
            halves = []
            for dn in range(2):
                xr = zf_scr[pl.ds(n2 + dn, DFT_N1, stride=DFT_N2_PITCH), :]
                xi = zf_scr[pl.ds(im_rows + n2 + dn, DFT_N1, stride=DFT_N2_PITCH), :]
                halves.append(jnp.concatenate([xr, xi], axis=0))
            x = jnp.concatenate(halves, axis=1).astype(BF16)
            a = jnp.dot(m_ref[...], x, preferred_element_type=F32)
            for dn in range(2):
                a_scr[pl.ds(n2 + dn, 2 * DFT_N1, stride=DFT_N2_PITCH), :] = (
                    a[:, dn * c:(dn + 1) * c])
        return carry

    lax.fori_loop(0, DFT_N2 // (2 * DFT_STAGE1_UNROLL), stage1, 0)

    def stage2(i, carry):
        for u in range(DFT_STAGE2_UNROLL):
            k1 = i * DFT_STAGE2_UNROLL + u
            r = pl.multiple_of(k1 * 2 * DFT_N2_PITCH, 8)
            rows = jnp.concatenate([a_scr[pl.ds(r, DFT_N2), :],
                                    a_scr[pl.ds(r + DFT_N2_PITCH, DFT_N2), :]],
                                   axis=0).astype(BF16)
            f = jnp.dot(t_ref[k1], rows, preferred_element_type=F32)
            f_scr[pl.ds(k1, DFT_N2, stride=DFT_N1_PITCH), :] = f
        return carry

    lax.fori_loop(0, DFT_N1 // DFT_STAGE2_UNROLL, stage2, 0)

    def gate(k2, carry):
        src = pl.multiple_of(k2 * DFT_N1_PITCH, 8)
        dst = pl.multiple_of(k2 * DFT_N1, DFT_N1)
        o_ref[0, pl.ds(dst, DFT_N1), :] = (
            f_scr[pl.ds(src, DFT_N1), :]
            * sg_ref[0, pl.ds(dst, DFT_N1), :].astype(F32)).astype(BF16)
        return carry

    lax.fori_loop(0, DFT_N2, gate, 0)


def _seq_dft_gate(z, sg, m1, t12):
    b, _, s, d = z.shape
    c = DFT_CH_TILE
    return pl.pallas_call(
        _seq_dft_body,
        grid=(b, d // c),
        in_specs=[
            pl.BlockSpec((1, 2, s, c), lambda bi, ci: (bi, 0, 0, ci)),
            pl.BlockSpec((1, s, c), lambda bi, ci: (bi, 0, ci)),
            _resident(m1.shape),
            _resident(t12.shape),
        ],
        out_specs=pl.BlockSpec((1, s, c), lambda bi, ci: (bi, 0, ci)),
        out_shape=jax.ShapeDtypeStruct((b, s, d), BF16),
        scratch_shapes=[
            pltpu.VMEM((2 * DFT_N1 * DFT_N2_PITCH, c), F32),
            pltpu.VMEM((2 * DFT_N1 * DFT_N2_PITCH, c), F32),
            pltpu.VMEM((DFT_N2 * DFT_N1_PITCH, c), F32),
        ],
        compiler_params=_params("arbitrary", "arbitrary"),
        name="seq_dft_gate",
    )(z, sg, m1, t12)


def _final_body(a_ref, w_ref, x_ref, gm_ref, fg_ref, o_ref):
    y = jnp.dot(a_ref[...], w_ref[...], preferred_element_type=F32)
    x2 = x_ref[...] + gm_ref[0] * y
    ms = jnp.mean(x2 * x2, axis=-1, keepdims=True)
    o_ref[...] = x2 * lax.rsqrt(ms + EPS) * fg_ref[...]


def _fourier_outproj_final(a2d, w_bf16, x2d, gate_mod, final_g):
    t, d = x2d.shape
    tiles_per_batch = SEQ // ROW_TILE
    return pl.pallas_call(
        _final_body,
        grid=(t // ROW_TILE,),
        in_specs=[
            pl.BlockSpec((ROW_TILE, a2d.shape[1]), lambda i: (i, 0)),
            _resident(w_bf16.shape),
            pl.BlockSpec((ROW_TILE, d), lambda i: (i, 0)),
            pl.BlockSpec((1, 1, d), lambda i: (i // tiles_per_batch, 0, 0)),
            _resident((1, d)),
        ],
        out_specs=pl.BlockSpec((ROW_TILE, d), lambda i: (i, 0)),
        out_shape=jax.ShapeDtypeStruct((t, d), F32),
        compiler_params=_params("arbitrary"),
        name="fourier_outproj_final",
    )(a2d, w_bf16, x2d, gate_mod, final_g)


def _rope_tables():
    rows = SEQ // GRID_W
    row_ids = jnp.repeat(jnp.arange(rows), GRID_W).astype(F32)
    col_ids = jnp.tile(jnp.arange(GRID_W), rows).astype(F32)
    inv_freq = ROPE_THETA ** (-jnp.arange(0, ROPE_AXIS_DIM, 2, dtype=F32) / ROPE_AXIS_DIM)
    ang = jnp.concatenate([row_ids[:, None] * inv_freq[None, :],
                           col_ids[:, None] * inv_freq[None, :]], axis=-1)
    cos, sin = jnp.cos(ang), jnp.sin(ang)
    return jnp.concatenate([cos, cos], axis=-1), jnp.concatenate([-sin, sin], axis=-1)


def _dft_tables():
    w = np.arange(FOURIER_GROUP_W)
    ang_w = 2.0 * np.pi * (np.outer(w, w) % FOURIER_GROUP_W) / FOURIER_GROUP_W
    cs = np.concatenate([np.cos(ang_w), -np.sin(ang_w)], axis=1) * 2.0 ** -4

    n1 = np.arange(DFT_N1)
    ang1 = 2.0 * np.pi * (np.outer(n1, n1) % DFT_N1) / DFT_N1
    c1, s1 = np.cos(ang1), np.sin(ang1)
    m1 = np.stack([np.concatenate([c1, s1], axis=1),
                   np.concatenate([-s1, c1], axis=1)], axis=1).reshape(2 * DFT_N1, 2 * DFT_N1)
    m1 = m1 * 2.0 ** -3

    k = n1[:, None] + DFT_N1 * np.arange(DFT_N2)[None, :]
    n2 = np.arange(DFT_N2)
    ang2 = 2.0 * np.pi * ((k[:, :, None] * n2[None, None, :]) % SEQ) / SEQ
    t12 = np.concatenate([np.cos(ang2), np.sin(ang2)], axis=2) * 2.0 ** -3.5
    as_bf16 = lambda a: jnp.asarray(a, dtype=F32).astype(BF16)
    return as_bf16(cs), as_bf16(m1), as_bf16(t12)


def kernel(x, c, norm_g, ada_w, ada_b, attn_w_in, attn_q_gain, attn_k_gain, attn_w_out,
           fourier_w_in, fourier_w_out, final_g):
    b, s, d = x.shape
    t = b * s

    c_pad = jnp.zeros((8, d), F32).at[:b].set(c)
    mod = _adaln(c_pad, ada_w, ada_b)[:, :b]
    shift = mod[:, :, None, :d]
    scale = mod[:, :, None, d:2 * d]
    gate = mod[:, :, None, 2 * d:]

    perm = np.concatenate([np.arange(0, HEAD_DIM, 2), np.arange(1, HEAD_DIM, 2)])
    n_qk_heads = N_HEADS + N_KV_HEADS
    qk_cols = (np.arange(n_qk_heads)[:, None] * HEAD_DIM + perm[None, :]).reshape(-1)
    cols = np.concatenate([qk_cols, np.arange(n_qk_heads * HEAD_DIM, ATTN_IN_WIDTH)])
    w_in0 = attn_w_in[0][:, cols].astype(BF16)
    q_gain = attn_q_gain[0][perm] * (math.log2(math.e) / math.sqrt(HEAD_DIM))
    k_gain = attn_k_gain[0][perm]
    gain_vec = jnp.concatenate([jnp.tile(q_gain, N_HEADS), jnp.tile(k_gain, N_KV_HEADS),
                                jnp.ones((ATTN_IN_WIDTH - n_qk_heads * HEAD_DIM,), F32)])[None, :]
    cos_full, sin_signed = _rope_tables()

    x2d = x.reshape(t, d)
    proj = _attn_inproj(x2d, norm_g[0][None, :], scale[0], shift[0], w_in0, gain_vec,
                        cos_full, sin_signed)
    score_bound = (HEAD_DIM * BF16_NORM_SLACK * jnp.max(jnp.abs(q_gain)) * jnp.max(jnp.abs(k_gain)))
    bounded = (score_bound <= SCORE_BOUND_LOG2).astype(jnp.int32).reshape(1)
    og = _attention(bounded, proj.reshape(b, s, ATTN_IN_WIDTH))
    x1 = _outproj_residual(og.reshape(t, ATTN_WIDTH), attn_w_out[0].astype(BF16), x2d, gate[0])

    cs, m1, t12 = _dft_tables()
    z, sg = _fourier_inproj(x1, norm_g[1][None, :], scale[1], shift[1],
                            fourier_w_in[0].astype(BF16), cs)
    fgate = _seq_dft_gate(z, sg, m1, t12)
    out = _fourier_outproj_final(fgate.reshape(t, d), fourier_w_out[0].astype(BF16), x1,
                                 gate[1], final_g[None, :])
    return out.reshape(b, s, d)
```

```python
import math

import numpy as np
import jax
import jax.numpy as jnp
from jax import lax
from jax.experimental import pallas as pl
from jax.experimental.pallas import tpu as pltpu

D_MODEL = 2048
BATCH = 4
SEQ = 8192
GRID_W = 64
HEAD_DIM = 128
N_HEADS = 16
N_KV_HEADS = 4
GQA_GROUP = N_HEADS // N_KV_HEADS
ATTN_WIDTH = N_HEADS * HEAD_DIM
KV_WIDTH = N_KV_HEADS * HEAD_DIM
ATTN_IN_WIDTH = 2 * ATTN_WIDTH + 2 * KV_WIDTH
ROPE_AXIS_DIM = HEAD_DIM // 2
ROPE_THETA = 10000.0
FOURIER_GROUPS = 8
FOURIER_GROUP_W = D_MODEL // FOURIER_GROUPS
EPS = 1e-6

DFT_N1 = 128
DFT_N2 = SEQ // DFT_N1

F32 = jnp.float32
BF16 = jnp.bfloat16

SCORE_BOUND_LOG2 = 64.0
BF16_NORM_SLACK = 1.01

VMEM_LIMIT_BYTES = 56 * 1024 * 1024

ROW_TILE = 512
COL_TILE = 512
ADALN_COL_TILE = 1024
Q_TILE = 256
KV_TILE = 256
BOUNDED_CHUNKS_PER_TRIP = 32
ONLINE_CHUNKS_PER_TRIP = 4
DFT_CH_TILE = 128
DFT_N2_PITCH = DFT_N2 + 8
DFT_N1_PITCH = DFT_N1 + 8
DFT_WIDEN_UNROLL = 8
DFT_STAGE1_UNROLL = 8
DFT_STAGE2_UNROLL = 16


def _params(*semantics):
    return pltpu.CompilerParams(dimension_semantics=semantics,
                                vmem_limit_bytes=VMEM_LIMIT_BYTES)


def _silu(x):
    return x * jax.nn.sigmoid(x)


def _adaln_body(c_ref, w_ref, b_ref, o_ref):
    c_act = _silu(c_ref[...])
    o_ref[0] = jnp.dot(c_act, w_ref[0], preferred_element_type=F32,
                       precision=lax.Precision.HIGHEST) + b_ref[0]


def _adaln(c_pad, ada_w, ada_b):
    depth, d, n = ada_w.shape
    rows = c_pad.shape[0]
    return pl.pallas_call(
        _adaln_body,
        grid=(depth, n // ADALN_COL_TILE),
        in_specs=[
            pl.BlockSpec((rows, d), lambda l, j: (0, 0)),
            pl.BlockSpec((1, d, ADALN_COL_TILE), lambda l, j: (l, 0, j)),
            pl.BlockSpec((1, 1, ADALN_COL_TILE), lambda l, j: (l, 0, j)),
        ],
        out_specs=pl.BlockSpec((1, rows, ADALN_COL_TILE), lambda l, j: (l, 0, j)),
        out_shape=jax.ShapeDtypeStruct((depth, rows, n), F32),
        compiler_params=_params("arbitrary", "arbitrary"),
        name="adaln",
    )(c_pad, ada_w, ada_b.reshape(depth, 1, n))


def _modulated_norm(x_ref, g_ref, sc_ref, sh_ref):
    x = x_ref[...]
    ms = jnp.mean(x * x, axis=-1, keepdims=True)
    a = g_ref[...] * (1.0 + sc_ref[0])
    return (x * lax.rsqrt(ms + EPS) * a + sh_ref[0]).astype(BF16)


def _attn_inproj_body(x_ref, g_ref, sc_ref, sh_ref, w_ref, gain_ref, cos_ref, sina_ref, sinb_ref,
                      qt_ref, k_ref, vt_ref, sg_ref):
    h = _modulated_norm(x_ref, g_ref, sc_ref, sh_ref)
    cos = cos_ref[...]
    sin_a = sina_ref[...]
    sin_b = sinb_ref[...]
    heads_per_tile = COL_TILE // HEAD_DIM
    n_q_tiles = ATTN_WIDTH // COL_TILE
    n_qk_tiles = (ATTN_WIDTH + KV_WIDTH) // COL_TILE
    for j in range(ATTN_IN_WIDTH // COL_TILE):
        acc = jnp.dot(h, w_ref[:, j * COL_TILE:(j + 1) * COL_TILE], preferred_element_type=F32)
        for hh in range(heads_per_tile):
            xh = acc[:, hh * HEAD_DIM:(hh + 1) * HEAD_DIM]
            if j < n_qk_tiles:
                col0 = j * COL_TILE + hh * HEAD_DIM
                ms = jnp.mean(xh * xh, axis=-1, keepdims=True)
                xn = xh * lax.rsqrt(ms + EPS) * gain_ref[:, col0:col0 + HEAD_DIM]
                rot = (xn * cos + pltpu.roll(xn, HEAD_DIM - 1, axis=1) * sin_a
                       + pltpu.roll(xn, 1, axis=1) * sin_b)
                if j < n_q_tiles:
                    qt_ref[0, j * heads_per_tile + hh] = rot.T.astype(BF16)
                else:
                    k_ref[0, hh] = rot.astype(BF16)
            elif j == n_qk_tiles:
                vt_ref[0, hh] = xh.T.astype(BF16)
        if j > n_qk_tiles:
            g0 = (j - n_qk_tiles - 1) * COL_TILE
            sg_ref[0, :, g0:g0 + COL_TILE] = _silu(acc).astype(BF16)


def _resident(shape):
    return pl.BlockSpec(shape, lambda *_: (0,) * len(shape), pipeline_mode=pl.Buffered(1))


def _attn_inproj(x2d, norm_g, scale, shift, w_bf16, gain_vec, cos_il, sin_a, sin_b):
    t, d = x2d.shape
    n = w_bf16.shape[1]
    tpb = SEQ // ROW_TILE
    rope_spec = pl.BlockSpec((ROW_TILE, HEAD_DIM), lambda i: (i % tpb, 0))
    return pl.pallas_call(
        _attn_inproj_body,
        grid=(t // ROW_TILE,),
        in_specs=[
            pl.BlockSpec((ROW_TILE, d), lambda i: (i, 0)),
            _resident((1, d)),
            pl.BlockSpec((1, 1, d), lambda i: (i // tpb, 0, 0)),
            pl.BlockSpec((1, 1, d), lambda i: (i // tpb, 0, 0)),
            _resident((d, n)),
            _resident(gain_vec.shape),
            rope_spec, rope_spec, rope_spec,
        ],
        out_specs=[
            pl.BlockSpec((1, N_HEADS, HEAD_DIM, ROW_TILE), lambda i: (i // tpb, 0, 0, i % tpb)),
            pl.BlockSpec((1, N_KV_HEADS, ROW_TILE, HEAD_DIM), lambda i: (i // tpb, 0, i % tpb, 0)),
            pl.BlockSpec((1, N_KV_HEADS, HEAD_DIM, ROW_TILE), lambda i: (i // tpb, 0, 0, i % tpb)),
            pl.BlockSpec((1, ROW_TILE, ATTN_WIDTH), lambda i: (i // tpb, i % tpb, 0)),
        ],
        out_shape=[
            jax.ShapeDtypeStruct((BATCH, N_HEADS, HEAD_DIM, SEQ), BF16),
            jax.ShapeDtypeStruct((BATCH, N_KV_HEADS, SEQ, HEAD_DIM), BF16),
            jax.ShapeDtypeStruct((BATCH, N_KV_HEADS, HEAD_DIM, SEQ), BF16),
            jax.ShapeDtypeStruct((BATCH, SEQ, ATTN_WIDTH), BF16),
        ],
        compiler_params=_params("arbitrary"),
        name="attn_inproj",
    )(x2d, norm_g, scale, shift, w_bf16, gain_vec, cos_il, sin_a, sin_b)


def _attn_body(bounded_ref, qt_ref, k_ref, vt_ref, sg_ref, o_ref, qt_scr, m_scr, l_scr, acc_scr,
               st_scr):
    n_kv_steps = SEQ // KV_TILE

    for hh in range(GQA_GROUP):
        qt_scr[:, hh * Q_TILE:(hh + 1) * Q_TILE] = qt_ref[0, hh]
    m_scr[...] = jnp.full(m_scr.shape, -jnp.inf, F32)
    l_scr[...] = jnp.zeros(l_scr.shape, F32)
    acc_scr[...] = jnp.zeros(acc_scr.shape, F32)

    def scores(c):
        start = pl.multiple_of(c * KV_TILE, KV_TILE)
        return jnp.dot(k_ref[0, 0, pl.ds(start, KV_TILE), :], qt_scr[...],
                       preferred_element_type=F32)

    def weighted_values(p, c):
        start = pl.multiple_of(c * KV_TILE, KV_TILE)
        return jnp.dot(vt_ref[0, 0, :, pl.ds(start, KV_TILE)], p.astype(BF16),
                       preferred_element_type=F32)

    def consume_online(st, c):
        m_old = m_scr[...]
        m_new = jnp.maximum(m_old, jnp.max(st, axis=0, keepdims=True))
        alpha = jnp.exp2(m_old - m_new)
        p = jnp.exp2(st - m_new)
        l_scr[0:1] = alpha * l_scr[0:1] + jnp.sum(p, axis=0, keepdims=True)
        acc_scr[...] = alpha * acc_scr[...] + weighted_values(p, c)
        m_scr[...] = m_new

    def consume_bounded(st, c):
        p = jnp.exp2(st)
        l_scr[...] += jnp.sum(p.reshape(KV_TILE // 8, 8, p.shape[1]), axis=0)
        acc_scr[...] += weighted_values(p, c)

    def run(consume, chunks_per_trip):
        st_scr[...] = scores(0)

        def step(i, carry):
            first = i * chunks_per_trip
            st = st_scr[...]
            for j in range(chunks_per_trip):
                nxt = jnp.minimum(first + j + 1, n_kv_steps - 1)
                st_next = scores(nxt)
                consume(st, first + j)
                st = st_next
            st_scr[...] = st
            return carry

        lax.fori_loop(0, n_kv_steps // chunks_per_trip, step, 0)

    @pl.when(bounded_ref[0] != 0)
    def _():
        run(consume_bounded, BOUNDED_CHUNKS_PER_TRIP)

    @pl.when(bounded_ref[0] == 0)
    def _():
        run(consume_online, ONLINE_CHUNKS_PER_TRIP)

    ot = acc_scr[...] / jnp.sum(l_scr[...], axis=0, keepdims=True)
    for hh in range(GQA_GROUP):
        sl = slice(hh * HEAD_DIM, (hh + 1) * HEAD_DIM)
        o = ot[:, hh * Q_TILE:(hh + 1) * Q_TILE].T
        o_ref[0, :, sl] = (o * sg_ref[0, :, sl].astype(F32)).astype(BF16)


def _attention(bounded, qt, k, vt, sg):
    b, _, _, s = qt.shape
    group_w = GQA_GROUP * HEAD_DIM
    nq = GQA_GROUP * Q_TILE
    return pl.pallas_call(
        _attn_body,
        grid=(b, N_KV_HEADS, s // Q_TILE),
        in_specs=[
            pl.BlockSpec(memory_space=pltpu.SMEM),
            pl.BlockSpec((1, GQA_GROUP, HEAD_DIM, Q_TILE), lambda bi, kh, qi: (bi, kh, 0, qi)),
            pl.BlockSpec((1, 1, s, HEAD_DIM), lambda bi, kh, qi: (bi, kh, 0, 0)),
            pl.BlockSpec((1, 1, HEAD_DIM, s), lambda bi, kh, qi: (bi, kh, 0, 0)),
            pl.BlockSpec((1, Q_TILE, group_w), lambda bi, kh, qi: (bi, qi, kh)),
        ],
        out_specs=pl.BlockSpec((1, Q_TILE, group_w), lambda bi, kh, qi: (bi, qi, kh)),
        out_shape=jax.ShapeDtypeStruct((b, s, ATTN_WIDTH), BF16),
        scratch_shapes=[
            pltpu.VMEM((HEAD_DIM, nq), BF16),
            pltpu.VMEM((1, nq), F32),
            pltpu.VMEM((8, nq), F32),
            pltpu.VMEM((HEAD_DIM, nq), F32),
            pltpu.VMEM((KV_TILE, nq), F32),
        ],
        compiler_params=_params("arbitrary", "arbitrary", "arbitrary"),
        name="attention",
    )(bounded, qt, k, vt, sg)


def _outproj_body(a_ref, w_ref, x_ref, gm_ref, o_ref):
    y = jnp.dot(a_ref[...], w_ref[...], preferred_element_type=F32)
    o_ref[...] = x_ref[...] + gm_ref[0] * y


def _outproj_residual(a2d, w_bf16, x2d, gate_mod):
    t, d = x2d.shape
    tiles_per_batch = SEQ // ROW_TILE
    return pl.pallas_call(
        _outproj_body,
        grid=(t // ROW_TILE,),
        in_specs=[
            pl.BlockSpec((ROW_TILE, a2d.shape[1]), lambda i: (i, 0)),
            pl.BlockSpec(w_bf16.shape, lambda i: (0, 0)),
            pl.BlockSpec((ROW_TILE, d), lambda i: (i, 0)),
            pl.BlockSpec((1, 1, d), lambda i: (i // tiles_per_batch, 0, 0)),
        ],
        out_specs=pl.BlockSpec((ROW_TILE, d), lambda i: (i, 0)),
        out_shape=jax.ShapeDtypeStruct((t, d), F32),
        compiler_params=_params("arbitrary"),
        name="attn_outproj",
    )(a2d, w_bf16, x2d, gate_mod)


def _fourier_inproj_body(x_ref, g_ref, sc_ref, sh_ref, w_ref, cs_ref, z_ref, sg_ref):
    h = _modulated_norm(x_ref, g_ref, sc_ref, sh_ref)
    n_u_tiles = D_MODEL // COL_TILE
    for j in range(2 * n_u_tiles):
        acc = jnp.dot(h, w_ref[:, j * COL_TILE:(j + 1) * COL_TILE], preferred_element_type=F32)
        if j < n_u_tiles:
            for gg in range(COL_TILE // FOURIER_GROUP_W):
                sl = slice(j * COL_TILE + gg * FOURIER_GROUP_W,
                           j * COL_TILE + (gg + 1) * FOURIER_GROUP_W)
                u = acc[:, gg * FOURIER_GROUP_W:(gg + 1) * FOURIER_GROUP_W].astype(BF16)
                z = jnp.dot(u, cs_ref[...], preferred_element_type=F32)
                z_ref[0, 0, :, sl] = z[:, :FOURIER_GROUP_W].astype(BF16)
                z_ref[0, 1, :, sl] = z[:, FOURIER_GROUP_W:].astype(BF16)
        else:
            cols = slice((j - n_u_tiles) * COL_TILE, (j - n_u_tiles + 1) * COL_TILE)
            sg_ref[0, :, cols] = _silu(acc).astype(BF16)


def _fourier_inproj(x2d, norm_g, scale, shift, w_bf16, cs):
    t, d = x2d.shape
    tiles_per_batch = SEQ // ROW_TILE
    return pl.pallas_call(
        _fourier_inproj_body,
        grid=(t // ROW_TILE,),
        in_specs=[
            pl.BlockSpec((ROW_TILE, d), lambda i: (i, 0)),
            _resident((1, d)),
            pl.BlockSpec((1, 1, d), lambda i: (i // tiles_per_batch, 0, 0)),
            pl.BlockSpec((1, 1, d), lambda i: (i // tiles_per_batch, 0, 0)),
            _resident(w_bf16.shape),
            _resident(cs.shape),
        ],
        out_specs=[
            pl.BlockSpec((1, 2, ROW_TILE, D_MODEL),
                         lambda i: (i // tiles_per_batch, 0, i % tiles_per_batch, 0)),
            pl.BlockSpec((1, ROW_TILE, D_MODEL),
                         lambda i: (i // tiles_per_batch, i % tiles_per_batch, 0)),
        ],
        out_shape=[jax.ShapeDtypeStruct((BATCH, 2, SEQ, D_MODEL), BF16),
                   jax.ShapeDtypeStruct((BATCH, SEQ, D_MODEL), BF16)],
        compiler_params=_params("arbitrary"),
        name="fourier_inproj",
    )(x2d, norm_g, scale, shift, w_bf16, cs)


def _seq_dft_body(z_ref, sg_ref, m_ref, t_ref, o_ref, zf_scr, a_scr, f_scr):
    c = DFT_CH_TILE
    im_rows = DFT_N1 * DFT_N2_PITCH

    def widen(i, carry):
        for u in range(DFT_WIDEN_UNROLL):
            n1 = i * DFT_WIDEN_UNROLL + u
            src = pl.multiple_of(n1 * DFT_N2, DFT_N2)
            dst = pl.multiple_of(n1 * DFT_N2_PITCH, 8)
            zf_scr[pl.ds(dst, DFT_N2), :] = z_ref[0, 0, pl.ds(src, DFT_N2), :].astype(F32)
            zf_scr[pl.ds(im_rows + dst, DFT_N2), :] = (
                z_ref[0, 1, pl.ds(src, DFT_N2), :].astype(F32))
        return carry

    lax.fori_loop(0, DFT_N1 // DFT_WIDEN_UNROLL, widen, 0)

    def stage1(i, carry):
        for u in range(DFT_STAGE1_UNROLL):
            n2 = (i * DFT_STAGE1_UNROLL + u) * 2
            halves = []
            for dn in range(2):
                xr = zf_scr[pl.ds(n2 + dn, DFT_N1, stride=DFT_N2_PITCH), :]
                xi = zf_scr[pl.ds(im_rows + n2 + dn, DFT_N1, stride=DFT_N2_PITCH), :]
                halves.append(jnp.concatenate([xr, xi], axis=0))
            x = jnp.concatenate(halves, axis=1).astype(BF16)
            a = jnp.dot(m_ref[...], x, preferred_element_type=F32)
            for dn in range(2):
                a_scr[pl.ds(n2 + dn, 2 * DFT_N1, stride=DFT_N2_PITCH), :] = (
                    a[:, dn * c:(dn + 1) * c])
        return carry

    lax.fori_loop(0, DFT_N2 // (2 * DFT_STAGE1_UNROLL), stage1, 0)

    def stage2(i, carry):
        for u in range(DFT_STAGE2_UNROLL):
            k1 = i * DFT_STAGE2_UNROLL + u
            r = pl.multiple_of(k1 * 2 * DFT_N2_PITCH, 8)
            rows = jnp.concatenate([a_scr[pl.ds(r, DFT_N2), :],
                                    a_scr[pl.ds(r + DFT_N2_PITCH, DFT_N2), :]],
                                   axis=0).astype(BF16)
            f = jnp.dot(t_ref[k1], rows, preferred_element_type=F32)
            f_scr[pl.ds(k1, DFT_N2, stride=DFT_N1_PITCH), :] = f
        return carry

    lax.fori_loop(0, DFT_N1 // DFT_STAGE2_UNROLL, stage2, 0)

    def gate(k2, carry):
        src = pl.multiple_of(k2 * DFT_N1_PITCH, 8)
        dst = pl.multiple_of(k2 * DFT_N1, DFT_N1)
        o_ref[0, pl.ds(dst, DFT_N1), :] = (
            f_scr[pl.ds(src, DFT_N1), :]
            * sg_ref[0, pl.ds(dst, DFT_N1), :].astype(F32)).astype(BF16)
        return carry

    lax.fori_loop(0, DFT_N2, gate, 0)


def _seq_dft_gate(z, sg, m1, t12):
    b, _, s, d = z.shape
    c = DFT_CH_TILE
    return pl.pallas_call(
        _seq_dft_body,
        grid=(b, d // c),
        in_specs=[
            pl.BlockSpec((1, 2, s, c), lambda bi, ci: (bi, 0, 0, ci)),
            pl.BlockSpec((1, s, c), lambda bi, ci: (bi, 0, ci)),
            _resident(m1.shape),
            _resident(t12.shape),
        ],
        out_specs=pl.BlockSpec((1, s, c), lambda bi, ci: (bi, 0, ci)),
        out_shape=jax.ShapeDtypeStruct((b, s, d), BF16),
        scratch_shapes=[
            pltpu.VMEM((2 * DFT_N1 * DFT_N2_PITCH, c), F32),
            pltpu.VMEM((2 * DFT_N1 * DFT_N2_PITCH, c), F32),
            pltpu.VMEM((DFT_N2 * DFT_N1_PITCH, c), F32),
        ],
        compiler_params=_params("arbitrary", "arbitrary"),
        name="seq_dft_gate",
    )(z, sg, m1, t12)


def _final_body(a_ref, w_ref, x_ref, gm_ref, fg_ref, o_ref):
    y = jnp.dot(a_ref[...], w_ref[...], preferred_element_type=F32)
    x2 = x_ref[...] + gm_ref[0] * y
    ms = jnp.mean(x2 * x2, axis=-1, keepdims=True)
    o_ref[...] = x2 * lax.rsqrt(ms + EPS) * fg_ref[...]


def _fourier_outproj_final(a2d, w_bf16, x2d, gate_mod, final_g):
    t, d = x2d.shape
    tiles_per_batch = SEQ // ROW_TILE
    return pl.pallas_call(
        _final_body,
        grid=(t // ROW_TILE,),
        in_specs=[
            pl.BlockSpec((ROW_TILE, a2d.shape[1]), lambda i: (i, 0)),
            _resident(w_bf16.shape),
            pl.BlockSpec((ROW_TILE, d), lambda i: (i, 0)),
            pl.BlockSpec((1, 1, d), lambda i: (i // tiles_per_batch, 0, 0)),
            _resident((1, d)),
        ],
        out_specs=pl.BlockSpec((ROW_TILE, d), lambda i: (i, 0)),
        out_shape=jax.ShapeDtypeStruct((t, d), F32),
        compiler_params=_params("arbitrary"),
        name="fourier_outproj_final",
    )(a2d, w_bf16, x2d, gate_mod, final_g)


def _rope_tables():
    rows = SEQ // GRID_W
    inv_freq = ROPE_THETA ** (-jnp.arange(0, ROPE_AXIS_DIM, 2, dtype=F32) / ROPE_AXIS_DIM)
    row_ang = jnp.arange(rows).astype(F32)[:, None] * inv_freq[None, :]
    col_ang = jnp.arange(GRID_W).astype(F32)[:, None] * inv_freq[None, :]
    n_freq = inv_freq.shape[0]

    def per_position(fn):
        by_row = jnp.broadcast_to(fn(row_ang)[:, None, :], (rows, GRID_W, n_freq))
        by_col = jnp.broadcast_to(fn(col_ang)[None, :, :], (rows, GRID_W, n_freq))
        return jnp.concatenate([by_row, by_col], axis=-1).reshape(SEQ, 2 * n_freq)

    cos, sin = per_position(jnp.cos), per_position(jnp.sin)
    zero = jnp.zeros_like(sin)
    pairs = lambda even, odd: jnp.stack([even, odd], axis=-1).reshape(SEQ, HEAD_DIM)
    return pairs(cos, cos), pairs(-sin, zero), pairs(zero, sin)


def _dft_tables():
    w = np.arange(FOURIER_GROUP_W)
    ang_w = 2.0 * np.pi * (np.outer(w, w) % FOURIER_GROUP_W) / FOURIER_GROUP_W
    cs = np.concatenate([np.cos(ang_w), -np.sin(ang_w)], axis=1) * 2.0 ** -4

    n1 = np.arange(DFT_N1)
    ang1 = 2.0 * np.pi * (np.outer(n1, n1) % DFT_N1) / DFT_N1
    c1, s1 = np.cos(ang1), np.sin(ang1)
    m1 = np.stack([np.concatenate([c1, s1], axis=1),
                   np.concatenate([-s1, c1], axis=1)], axis=1).reshape(2 * DFT_N1, 2 * DFT_N1)
    m1 = m1 * 2.0 ** -3

    k = n1[:, None] + DFT_N1 * np.arange(DFT_N2)[None, :]
    n2 = np.arange(DFT_N2)
    ang2 = 2.0 * np.pi * ((k[:, :, None] * n2[None, None, :]) % SEQ) / SEQ
    t12 = np.concatenate([np.cos(ang2), np.sin(ang2)], axis=2) * 2.0 ** -3.5
    as_bf16 = lambda a: jnp.asarray(a, dtype=F32).astype(BF16)
    return as_bf16(cs), as_bf16(m1), as_bf16(t12)


def kernel(x, c, norm_g, ada_w, ada_b, attn_w_in, attn_q_gain, attn_k_gain, attn_w_out,
           fourier_w_in, fourier_w_out, final_g):
    b, s, d = x.shape
    t = b * s

    c_pad = jnp.zeros((8, d), F32).at[:b].set(c)
    mod = _adaln(c_pad, ada_w, ada_b)[:, :b]
    shift = mod[:, :, None, :d]
    scale = mod[:, :, None, d:2 * d]
    gate = mod[:, :, None, 2 * d:]

    q_gain = attn_q_gain[0] * (math.log2(math.e) / math.sqrt(HEAD_DIM))
    k_gain = attn_k_gain[0]
    gain_vec = jnp.concatenate([jnp.tile(q_gain, N_HEADS), jnp.tile(k_gain, N_KV_HEADS)])[None, :]
    cos_il, sin_a, sin_b = _rope_tables()

    x2d = x.reshape(t, d)
    qt, k, vt, sg0 = _attn_inproj(x2d, norm_g[0][None, :], scale[0], shift[0],
                                  attn_w_in[0].astype(BF16), gain_vec, cos_il, sin_a, sin_b)
    score_bound = (HEAD_DIM * BF16_NORM_SLACK * jnp.max(jnp.abs(q_gain)) * jnp.max(jnp.abs(k_gain)))
    bounded = (score_bound <= SCORE_BOUND_LOG2).astype(jnp.int32).reshape(1)
    og = _attention(bounded, qt, k, vt, sg0)
    x1 = _outproj_residual(og.reshape(t, ATTN_WIDTH), attn_w_out[0].astype(BF16), x2d, gate[0])

    cs, m1, t12 = _dft_tables()
    z, sg = _fourier_inproj(x1, norm_g[1][None, :], scale[1], shift[1],
                            fourier_w_in[0].astype(BF16), cs)
    fgate = _seq_dft_gate(z, sg, m1, t12)
    out = _fourier_outproj_final(fgate.reshape(t, d), fourier_w_out[0].astype(BF16), x1,
                                 gate[1], final_g[None, :])
    return out.reshape(b, s, d)
```

```python
import math

import numpy as np
import jax
import jax.numpy as jnp
from jax import lax
from jax.experimental import pallas as pl
from jax.experimental.pallas import tpu as pltpu

D_MODEL = 2048
BATCH = 4
SEQ = 8192
GRID_W = 64
HEAD_DIM = 128
N_HEADS = 16
N_KV_HEADS = 4
GQA_GROUP = N_HEADS // N_KV_HEADS
ATTN_WIDTH = N_HEADS * HEAD_DIM
KV_WIDTH = N_KV_HEADS * HEAD_DIM
ATTN_IN_WIDTH = 2 * ATTN_WIDTH + 2 * KV_WIDTH
ROPE_AXIS_DIM = HEAD_DIM // 2
ROPE_THETA = 10000.0
FOURIER_GROUPS = 8
FOURIER_GROUP_W = D_MODEL // FOURIER_GROUPS
EPS = 1e-6

DFT_N1 = 128
DFT_N2 = SEQ // DFT_N1

F32 = jnp.float32
BF16 = jnp.bfloat16

SCORE_BOUND_LOG2 = 64.0
BF16_NORM_SLACK = 1.01

VMEM_LIMIT_BYTES = 56 * 1024 * 1024

ROW_TILE = 512
ATTN_ROW_SUBTILE = 256
FOURIER_ROW_SUBTILE = 512
COL_TILE = 512
ADALN_COL_TILE = 1024
Q_TILE = 256
KV_TILE = 256
BOUNDED_CHUNKS_PER_TRIP = 32
ONLINE_CHUNKS_PER_TRIP = 4
DFT_CH_TILE = 128
DFT_N2_PITCH = DFT_N2 + 8
DFT_N1_PITCH = DFT_N1 + 8
DFT_WIDEN_UNROLL = 8
DFT_STAGE1_UNROLL = 8
DFT_STAGE2_UNROLL = 16
DFT_GATE_UNROLL = 4


def _params(*semantics):
    return pltpu.CompilerParams(dimension_semantics=semantics,
                                vmem_limit_bytes=VMEM_LIMIT_BYTES)


def _silu(x):
    return x * jax.nn.sigmoid(x)


def _adaln_body(c_ref, w_ref, b_ref, o_ref):
    c_act = _silu(c_ref[...])
    o_ref[0] = jnp.dot(c_act, w_ref[0], preferred_element_type=F32,
                       precision=lax.Precision.HIGHEST) + b_ref[0]


def _adaln(c_pad, ada_w, ada_b):
    depth, d, n = ada_w.shape
    rows = c_pad.shape[0]
    return pl.pallas_call(
        _adaln_body,
        grid=(depth, n // ADALN_COL_TILE),
        in_specs=[
            pl.BlockSpec((rows, d), lambda l, j: (0, 0)),
            pl.BlockSpec((1, d, ADALN_COL_TILE), lambda l, j: (l, 0, j)),
            pl.BlockSpec((1, 1, ADALN_COL_TILE), lambda l, j: (l, 0, j)),
        ],
        out_specs=pl.BlockSpec((1, rows, ADALN_COL_TILE), lambda l, j: (l, 0, j)),
        out_shape=jax.ShapeDtypeStruct((depth, rows, n), F32),
        compiler_params=_params("arbitrary", "arbitrary"),
        name="adaln",
    )(c_pad, ada_w, ada_b.reshape(depth, 1, n))


def _modulated_norm(x_ref, g_ref, sc_ref, sh_ref, rows):
    x = x_ref[rows, :]
    ms = jnp.mean(x * x, axis=-1, keepdims=True)
    a = g_ref[...] * (1.0 + sc_ref[0])
    return (x * lax.rsqrt(ms + EPS) * a + sh_ref[0]).astype(BF16)


def _row_subtiles(rows_per_matmul):
    return [slice(r, r + rows_per_matmul) for r in range(0, ROW_TILE, rows_per_matmul)]


def _attn_inproj_body(x_ref, g_ref, sc_ref, sh_ref, w_ref, gain_ref, cos_ref, sina_ref, sinb_ref,
                      qt_ref, k_ref, vt_ref, sg_ref):
    heads_per_tile = COL_TILE // HEAD_DIM
    n_q_tiles = ATTN_WIDTH // COL_TILE
    n_qk_tiles = (ATTN_WIDTH + KV_WIDTH) // COL_TILE
    for rows in _row_subtiles(ATTN_ROW_SUBTILE):
        h = _modulated_norm(x_ref, g_ref, sc_ref, sh_ref, rows)
        cos = cos_ref[rows, :]
        sin_a = sina_ref[rows, :]
        sin_b = sinb_ref[rows, :]
        for j in range(ATTN_IN_WIDTH // COL_TILE):
            acc = jnp.dot(h, w_ref[:, j * COL_TILE:(j + 1) * COL_TILE],
                          preferred_element_type=F32)
            for hh in range(heads_per_tile):
                xh = acc[:, hh * HEAD_DIM:(hh + 1) * HEAD_DIM]
                if j < n_qk_tiles:
                    col0 = j * COL_TILE + hh * HEAD_DIM
                    ms = jnp.mean(xh * xh, axis=-1, keepdims=True)
                    xn = xh * lax.rsqrt(ms + EPS) * gain_ref[:, col0:col0 + HEAD_DIM]
                    rot = (xn * cos + pltpu.roll(xn, HEAD_DIM - 1, axis=1) * sin_a
                           + pltpu.roll(xn, 1, axis=1) * sin_b)
                    if j < n_q_tiles:
                        qt_ref[0, j * heads_per_tile + hh, :, rows] = rot.T.astype(BF16)
                    else:
                        k_ref[0, hh, rows, :] = rot.astype(BF16)
                elif j == n_qk_tiles:
                    vt_ref[0, hh, :, rows] = xh.T.astype(BF16)
            if j > n_qk_tiles:
                g0 = (j - n_qk_tiles - 1) * COL_TILE
                sg_ref[0, rows, g0:g0 + COL_TILE] = _silu(acc).astype(BF16)


def _resident(shape):
    return pl.BlockSpec(shape, lambda *_: (0,) * len(shape), pipeline_mode=pl.Buffered(1))


def _attn_inproj(x2d, norm_g, scale, shift, w_bf16, gain_vec, cos_il, sin_a, sin_b):
    t, d = x2d.shape
    n = w_bf16.shape[1]
    tpb = SEQ // ROW_TILE
    rope_spec = pl.BlockSpec((ROW_TILE, HEAD_DIM), lambda i: (i % tpb, 0))
    return pl.pallas_call(
        _attn_inproj_body,
        grid=(t // ROW_TILE,),
        in_specs=[
            pl.BlockSpec((ROW_TILE, d), lambda i: (i, 0)),
            _resident((1, d)),
            pl.BlockSpec((1, 1, d), lambda i: (i // tpb, 0, 0)),
            pl.BlockSpec((1, 1, d), lambda i: (i // tpb, 0, 0)),
            _resident((d, n)),
            _resident(gain_vec.shape),
            rope_spec, rope_spec, rope_spec,
        ],
        out_specs=[
            pl.BlockSpec((1, N_HEADS, HEAD_DIM, ROW_TILE), lambda i: (i // tpb, 0, 0, i % tpb)),
            pl.BlockSpec((1, N_KV_HEADS, ROW_TILE, HEAD_DIM), lambda i: (i // tpb, 0, i % tpb, 0)),
            pl.BlockSpec((1, N_KV_HEADS, HEAD_DIM, ROW_TILE), lambda i: (i // tpb, 0, 0, i % tpb)),
            pl.BlockSpec((1, ROW_TILE, ATTN_WIDTH), lambda i: (i // tpb, i % tpb, 0)),
        ],
        out_shape=[
            jax.ShapeDtypeStruct((BATCH, N_HEADS, HEAD_DIM, SEQ), BF16),
            jax.ShapeDtypeStruct((BATCH, N_KV_HEADS, SEQ, HEAD_DIM), BF16),
            jax.ShapeDtypeStruct((BATCH, N_KV_HEADS, HEAD_DIM, SEQ), BF16),
            jax.ShapeDtypeStruct((BATCH, SEQ, ATTN_WIDTH), BF16),
        ],
        compiler_params=_params("arbitrary"),
        name="attn_inproj",
    )(x2d, norm_g, scale, shift, w_bf16, gain_vec, cos_il, sin_a, sin_b)


def _attn_body(bounded_ref, qt_ref, k_ref, vt_ref, sg_ref, o_ref, qt_scr, m_scr, l_scr, acc_scr,
               st_scr):
    n_kv_steps = SEQ // KV_TILE

    for hh in range(GQA_GROUP):
        qt_scr[:, hh * Q_TILE:(hh + 1) * Q_TILE] = qt_ref[0, hh]
    m_scr[...] = jnp.full(m_scr.shape, -jnp.inf, F32)
    l_scr[...] = jnp.zeros(l_scr.shape, F32)
    acc_scr[...] = jnp.zeros(acc_scr.shape, F32)

    def chunk(c):
        start = c * KV_TILE if isinstance(c, int) else pl.multiple_of(c * KV_TILE, KV_TILE)
        return pl.ds(start, KV_TILE)

    def scores(c):
        return jnp.dot(k_ref[0, 0, chunk(c), :], qt_scr[...], preferred_element_type=F32)

    def weighted_values(p, c):
        return jnp.dot(vt_ref[0, 0, :, chunk(c)], p.astype(BF16), preferred_element_type=F32)

    def consume_online(st, c):
        m_old = m_scr[...]
        m_new = jnp.maximum(m_old, jnp.max(st, axis=0, keepdims=True))
        alpha = jnp.exp2(m_old - m_new)
        p = jnp.exp2(st - m_new)
        l_scr[0:1] = alpha * l_scr[0:1] + jnp.sum(p, axis=0, keepdims=True)
        acc_scr[...] = alpha * acc_scr[...] + weighted_values(p, c)
        m_scr[...] = m_new

    def consume_bounded(st, c):
        p = jnp.exp2(st)
        l_scr[...] += jnp.sum(p.reshape(KV_TILE // 8, 8, p.shape[1]), axis=0)
        acc_scr[...] += weighted_values(p, c)

    def run(consume, chunks_per_trip):
        if chunks_per_trip == n_kv_steps:
            st = scores(0)
            for j in range(n_kv_steps):
                st_next = scores(j + 1) if j + 1 < n_kv_steps else None
                consume(st, j)
                st = st_next
            return

        st_scr[...] = scores(0)

        def step(i, carry):
            first = i * chunks_per_trip
            st = st_scr[...]
            for j in range(chunks_per_trip):
                nxt = jnp.minimum(first + j + 1, n_kv_steps - 1)
                st_next = scores(nxt)
                consume(st, first + j)
                st = st_next
            st_scr[...] = st
            return carry

        lax.fori_loop(0, n_kv_steps // chunks_per_trip, step, 0)

    @pl.when(bounded_ref[0] != 0)
    def _():
        run(consume_bounded, BOUNDED_CHUNKS_PER_TRIP)

    @pl.when(bounded_ref[0] == 0)
    def _():
        run(consume_online, ONLINE_CHUNKS_PER_TRIP)

    ot = acc_scr[...] / jnp.sum(l_scr[...], axis=0, keepdims=True)
    for hh in range(GQA_GROUP):
        sl = slice(hh * HEAD_DIM, (hh + 1) * HEAD_DIM)
        o = ot[:, hh * Q_TILE:(hh + 1) * Q_TILE].T
        o_ref[0, :, sl] = (o * sg_ref[0, :, sl].astype(F32)).astype(BF16)


def _attention(bounded, qt, k, vt, sg):
    b, _, _, s = qt.shape
    group_w = GQA_GROUP * HEAD_DIM
    nq = GQA_GROUP * Q_TILE
    return pl.pallas_call(
        _attn_body,
        grid=(b, N_KV_HEADS, s // Q_TILE),
        in_specs=[
            pl.BlockSpec(memory_space=pltpu.SMEM),
            pl.BlockSpec((1, GQA_GROUP, HEAD_DIM, Q_TILE), lambda bi, kh, qi: (bi, kh, 0, qi)),
            pl.BlockSpec((1, 1, s, HEAD_DIM), lambda bi, kh, qi: (bi, kh, 0, 0)),
            pl.BlockSpec((1, 1, HEAD_DIM, s), lambda bi, kh, qi: (bi, kh, 0, 0)),
            pl.BlockSpec((1, Q_TILE, group_w), lambda bi, kh, qi: (bi, qi, kh)),
        ],
        out_specs=pl.BlockSpec((1, Q_TILE, group_w), lambda bi, kh, qi: (bi, qi, kh)),
        out_shape=jax.ShapeDtypeStruct((b, s, ATTN_WIDTH), BF16),
        scratch_shapes=[
            pltpu.VMEM((HEAD_DIM, nq), BF16),
            pltpu.VMEM((1, nq), F32),
            pltpu.VMEM((8, nq), F32),
            pltpu.VMEM((HEAD_DIM, nq), F32),
            pltpu.VMEM((KV_TILE, nq), F32),
        ],
        compiler_params=_params("arbitrary", "arbitrary", "arbitrary"),
        name="attention",
    )(bounded, qt, k, vt, sg)


def _outproj_body(a_ref, w_ref, x_ref, gm_ref, o_ref):
    y = jnp.dot(a_ref[...], w_ref[...], preferred_element_type=F32)
    o_ref[...] = x_ref[...] + gm_ref[0] * y


def _outproj_residual(a2d, w_bf16, x2d, gate_mod):
    t, d = x2d.shape
    tiles_per_batch = SEQ // ROW_TILE
    return pl.pallas_call(
        _outproj_body,
        grid=(t // ROW_TILE,),
        in_specs=[
            pl.BlockSpec((ROW_TILE, a2d.shape[1]), lambda i: (i, 0)),
            pl.BlockSpec(w_bf16.shape, lambda i: (0, 0)),
            pl.BlockSpec((ROW_TILE, d), lambda i: (i, 0)),
            pl.BlockSpec((1, 1, d), lambda i: (i // tiles_per_batch, 0, 0)),
        ],
        out_specs=pl.BlockSpec((ROW_TILE, d), lambda i: (i, 0)),
        out_shape=jax.ShapeDtypeStruct((t, d), F32),
        compiler_params=_params("arbitrary"),
        name="attn_outproj",
    )(a2d, w_bf16, x2d, gate_mod)


def _fourier_inproj_body(x_ref, g_ref, sc_ref, sh_ref, w_ref, cs_ref, z_ref, sg_ref):
    n_u_tiles = D_MODEL // COL_TILE
    for rows in _row_subtiles(FOURIER_ROW_SUBTILE):
        h = _modulated_norm(x_ref, g_ref, sc_ref, sh_ref, rows)
        for j in range(2 * n_u_tiles):
            acc = jnp.dot(h, w_ref[:, j * COL_TILE:(j + 1) * COL_TILE],
                          preferred_element_type=F32)
            if j < n_u_tiles:
                for gg in range(COL_TILE // FOURIER_GROUP_W):
                    sl = slice(j * COL_TILE + gg * FOURIER_GROUP_W,
                               j * COL_TILE + (gg + 1) * FOURIER_GROUP_W)
                    u = acc[:, gg * FOURIER_GROUP_W:(gg + 1) * FOURIER_GROUP_W].astype(BF16)
                    z = jnp.dot(u, cs_ref[...], preferred_element_type=F32)
                    z_ref[0, 0, rows, sl] = z[:, :FOURIER_GROUP_W].astype(BF16)
                    z_ref[0, 1, rows, sl] = z[:, FOURIER_GROUP_W:].astype(BF16)
            else:
                cols = slice((j - n_u_tiles) * COL_TILE, (j - n_u_tiles + 1) * COL_TILE)
                sg_ref[0, rows, cols] = _silu(acc).astype(BF16)


def _fourier_inproj(x2d, norm_g, scale, shift, w_bf16, cs):
    t, d = x2d.shape
    tiles_per_batch = SEQ // ROW_TILE
    return pl.pallas_call(
        _fourier_inproj_body,
        grid=(t // ROW_TILE,),
        in_specs=[
            pl.BlockSpec((ROW_TILE, d), lambda i: (i, 0)),
            _resident((1, d)),
            pl.BlockSpec((1, 1, d), lambda i: (i // tiles_per_batch, 0, 0)),
            pl.BlockSpec((1, 1, d), lambda i: (i // tiles_per_batch, 0, 0)),
            _resident(w_bf16.shape),
            _resident(cs.shape),
        ],
        out_specs=[
            pl.BlockSpec((1, 2, ROW_TILE, D_MODEL),
                         lambda i: (i // tiles_per_batch, 0, i % tiles_per_batch, 0)),
            pl.BlockSpec((1, ROW_TILE, D_MODEL),
                         lambda i: (i // tiles_per_batch, i % tiles_per_batch, 0)),
        ],
        out_shape=[jax.ShapeDtypeStruct((BATCH, 2, SEQ, D_MODEL), BF16),
                   jax.ShapeDtypeStruct((BATCH, SEQ, D_MODEL), BF16)],
        compiler_params=_params("arbitrary"),
        name="fourier_inproj",
    )(x2d, norm_g, scale, shift, w_bf16, cs)


def _seq_dft_body(z_ref, sg_ref, m_ref, t_ref, o_ref, zf_scr, a_scr, f_scr):
    c = DFT_CH_TILE
    im_rows = DFT_N1 * DFT_N2_PITCH

    def widen(i, carry):
        for u in range(DFT_WIDEN_UNROLL):
            n1 = i * DFT_WIDEN_UNROLL + u
            src = pl.multiple_of(n1 * DFT_N2, DFT_N2)
            dst = pl.multiple_of(n1 * DFT_N2_PITCH, 8)
            zf_scr[pl.ds(dst, DFT_N2), :] = z_ref[0, 0, pl.ds(src, DFT_N2), :].astype(F32)
            zf_scr[pl.ds(im_rows + dst, DFT_N2), :] = (
                z_ref[0, 1, pl.ds(src, DFT_N2), :].astype(F32))
        return carry

    lax.fori_loop(0, DFT_N1 // DFT_WIDEN_UNROLL, widen, 0)

    def stage1(i, carry):
        for u in range(DFT_STAGE1_UNROLL):
            n2 = (i * DFT_STAGE1_UNROLL + u) * 2
            halves = []
            for dn in range(2):
                xr = zf_scr[pl.ds(n2 + dn, DFT_N1, stride=DFT_N2_PITCH), :]
                xi = zf_scr[pl.ds(im_rows + n2 + dn, DFT_N1, stride=DFT_N2_PITCH), :]
                halves.append(jnp.concatenate([xr, xi], axis=0))
            x = jnp.concatenate(halves, axis=1).astype(BF16)
            a = jnp.dot(m_ref[...], x, preferred_element_type=F32)
            for dn in range(2):
                a_scr[pl.ds(n2 + dn, 2 * DFT_N1, stride=DFT_N2_PITCH), :] = (
                    a[:, dn * c:(dn + 1) * c])
        return carry

    lax.fori_loop(0, DFT_N2 // (2 * DFT_STAGE1_UNROLL), stage1, 0)

    def stage2(i, carry):
        for u in range(DFT_STAGE2_UNROLL):
            k1 = i * DFT_STAGE2_UNROLL + u
            r = pl.multiple_of(k1 * 2 * DFT_N2_PITCH, 8)
            rows = jnp.concatenate([a_scr[pl.ds(r, DFT_N2), :],
                                    a_scr[pl.ds(r + DFT_N2_PITCH, DFT_N2), :]],
                                   axis=0).astype(BF16)
            f = jnp.dot(t_ref[k1], rows, preferred_element_type=F32)
            f_scr[pl.ds(k1, DFT_N2, stride=DFT_N1_PITCH), :] = f
        return carry

    lax.fori_loop(0, DFT_N1 // DFT_STAGE2_UNROLL, stage2, 0)

    def gate(i, carry):
        for u in range(DFT_GATE_UNROLL):
            k2 = i * DFT_GATE_UNROLL + u
            src = pl.multiple_of(k2 * DFT_N1_PITCH, 8)
            dst = pl.multiple_of(k2 * DFT_N1, DFT_N1)
            o_ref[0, pl.ds(dst, DFT_N1), :] = (
                f_scr[pl.ds(src, DFT_N1), :]
                * sg_ref[0, pl.ds(dst, DFT_N1), :].astype(F32)).astype(BF16)
        return carry

    lax.fori_loop(0, DFT_N2 // DFT_GATE_UNROLL, gate, 0)


def _seq_dft_gate(z, sg, m1, t12):
    b, _, s, d = z.shape
    c = DFT_CH_TILE
    return pl.pallas_call(
        _seq_dft_body,
        grid=(b, d // c),
        in_specs=[
            pl.BlockSpec((1, 2, s, c), lambda bi, ci: (bi, 0, 0, ci)),
            pl.BlockSpec((1, s, c), lambda bi, ci: (bi, 0, ci)),
            _resident(m1.shape),
            _resident(t12.shape),
        ],
        out_specs=pl.BlockSpec((1, s, c), lambda bi, ci: (bi, 0, ci)),
        out_shape=jax.ShapeDtypeStruct((b, s, d), BF16),
        scratch_shapes=[
            pltpu.VMEM((2 * DFT_N1 * DFT_N2_PITCH, c), F32),
            pltpu.VMEM((2 * DFT_N1 * DFT_N2_PITCH, c), F32),
            pltpu.VMEM((DFT_N2 * DFT_N1_PITCH, c), F32),
        ],
        compiler_params=_params("arbitrary", "arbitrary"),
        name="seq_dft_gate",
    )(z, sg, m1, t12)


def _final_body(a_ref, w_ref, x_ref, gm_ref, fg_ref, o_ref):
    y = jnp.dot(a_ref[...], w_ref[...], preferred_element_type=F32)
    x2 = x_ref[...] + gm_ref[0] * y
    ms = jnp.mean(x2 * x2, axis=-1, keepdims=True)
    o_ref[...] = x2 * lax.rsqrt(ms + EPS) * fg_ref[...]


def _fourier_outproj_final(a2d, w_bf16, x2d, gate_mod, final_g):
    t, d = x2d.shape
    tiles_per_batch = SEQ // ROW_TILE
    return pl.pallas_call(
        _final_body,
        grid=(t // ROW_TILE,),
        in_specs=[
            pl.BlockSpec((ROW_TILE, a2d.shape[1]), lambda i: (i, 0)),
            _resident(w_bf16.shape),
            pl.BlockSpec((ROW_TILE, d), lambda i: (i, 0)),
            pl.BlockSpec((1, 1, d), lambda i: (i // tiles_per_batch, 0, 0)),
            _resident((1, d)),
        ],
        out_specs=pl.BlockSpec((ROW_TILE, d), lambda i: (i, 0)),
        out_shape=jax.ShapeDtypeStruct((t, d), F32),
        compiler_params=_params("arbitrary"),
        name="fourier_outproj_final",
    )(a2d, w_bf16, x2d, gate_mod, final_g)


def _rope_tables():
    rows = SEQ // GRID_W
    inv_freq = ROPE_THETA ** (-jnp.arange(0, ROPE_AXIS_DIM, 2, dtype=F32) / ROPE_AXIS_DIM)
    row_ang = jnp.arange(rows).astype(F32)[:, None] * inv_freq[None, :]
    col_ang = jnp.arange(GRID_W).astype(F32)[:, None] * inv_freq[None, :]
    n_freq = inv_freq.shape[0]

    def per_position(fn):
        by_row = jnp.broadcast_to(fn(row_ang)[:, None, :], (rows, GRID_W, n_freq))
        by_col = jnp.broadcast_to(fn(col_ang)[None, :, :], (rows, GRID_W, n_freq))
        return jnp.concatenate([by_row, by_col], axis=-1).reshape(SEQ, 2 * n_freq)

    cos, sin = per_position(jnp.cos), per_position(jnp.sin)
    zero = jnp.zeros_like(sin)
    pairs = lambda even, odd: jnp.stack([even, odd], axis=-1).reshape(SEQ, HEAD_DIM)
    return pairs(cos, cos), pairs(-sin, zero), pairs(zero, sin)


def _dft_tables():
    w = np.arange(FOURIER_GROUP_W)
    ang_w = 2.0 * np.pi * (np.outer(w, w) % FOURIER_GROUP_W) / FOURIER_GROUP_W
    cs = np.concatenate([np.cos(ang_w), -np.sin(ang_w)], axis=1) * 2.0 ** -4

    n1 = np.arange(DFT_N1)
    ang1 = 2.0 * np.pi * (np.outer(n1, n1) % DFT_N1) / DFT_N1
    c1, s1 = np.cos(ang1), np.sin(ang1)
    m1 = np.stack([np.concatenate([c1, s1], axis=1),
                   np.concatenate([-s1, c1], axis=1)], axis=1).reshape(2 * DFT_N1, 2 * DFT_N1)
    m1 = m1 * 2.0 ** -3

    k = n1[:, None] + DFT_N1 * np.arange(DFT_N2)[None, :]
    n2 = np.arange(DFT_N2)
    ang2 = 2.0 * np.pi * ((k[:, :, None] * n2[None, None, :]) % SEQ) / SEQ
    t12 = np.concatenate([np.cos(ang2), np.sin(ang2)], axis=2) * 2.0 ** -3.5
    as_bf16 = lambda a: jnp.asarray(a, dtype=F32).astype(BF16)
    return as_bf16(cs), as_bf16(m1), as_bf16(t12)


def kernel(x, c, norm_g, ada_w, ada_b, attn_w_in, attn_q_gain, attn_k_gain, attn_w_out,
           fourier_w_in, fourier_w_out, final_g):
    b, s, d = x.shape
    t = b * s

    c_pad = jnp.zeros((8, d), F32).at[:b].set(c)
    mod = _adaln(c_pad, ada_w, ada_b)[:, :b]
    shift = mod[:, :, None, :d]
    scale = mod[:, :, None, d:2 * d]
    gate = mod[:, :, None, 2 * d:]

    q_gain = attn_q_gain[0] * (math.log2(math.e) / math.sqrt(HEAD_DIM))
    k_gain = attn_k_gain[0]
    gain_vec = jnp.concatenate([jnp.tile(q_gain, N_HEADS), jnp.tile(k_gain, N_KV_HEADS)])[None, :]
    cos_il, sin_a, sin_b = _rope_tables()

    x2d = x.reshape(t, d)
    qt, k, vt, sg0 = _attn_inproj(x2d, norm_g[0][None, :], scale[0], shift[0],
                                  attn_w_in[0].astype(BF16), gain_vec, cos_il, sin_a, sin_b)
    score_bound = (HEAD_DIM * BF16_NORM_SLACK * jnp.max(jnp.abs(q_gain)) * jnp.max(jnp.abs(k_gain)))
    bounded = (score_bound <= SCORE_BOUND_LOG2).astype(jnp.int32).reshape(1)
    og = _attention(bounded, qt, k, vt, sg0)
    x1 = _outproj_residual(og.reshape(t, ATTN_WIDTH), attn_w_out[0].astype(BF16), x2d, gate[0])

    cs, m1, t12 = _dft_tables()
    z, sg = _fourier_inproj(x1, norm_g[1][None, :], scale[1], shift[1],
                            fourier_w_in[0].astype(BF16), cs)
    fgate = _seq_dft_gate(z, sg, m1, t12)
    out = _fourier_outproj_final(fgate.reshape(t, d), fourier_w_out[0].astype(BF16), x1,
                                 gate[1], final_g[None, :])
    return out.reshape(b, s, d)
```

```python
import math

import numpy as np
import jax
import jax.numpy as jnp
from jax import lax
from jax.experimental import pallas as pl
from jax.experimental.pallas import tpu as pltpu

D_MODEL = 2048
BATCH = 4
SEQ = 8192
GRID_W = 64
HEAD_DIM = 128
N_HEADS = 16
N_KV_HEADS = 4
GQA_GROUP = N_HEADS // N_KV_HEADS
ATTN_WIDTH = N_HEADS * HEAD_DIM
KV_WIDTH = N_KV_HEADS * HEAD_DIM
ATTN_IN_WIDTH = 2 * ATTN_WIDTH + 2 * KV_WIDTH
ROPE_AXIS_DIM = HEAD_DIM // 2
ROPE_THETA = 10000.0
FOURIER_GROUPS = 8
FOURIER_GROUP_W = D_MODEL // FOURIER_GROUPS
EPS = 1e-6

DFT_N1 = 128
DFT_N2 = SEQ // DFT_N1

F32 = jnp.float32
BF16 = jnp.bfloat16

SCORE_BOUND_LOG2 = 64.0
BF16_NORM_SLACK = 1.01

VMEM_LIMIT_BYTES = 56 * 1024 * 1024

ROW_TILE = 512
ATTN_ROW_SUBTILE = 256
FOURIER_ROW_SUBTILE = 512
COL_TILE = 512
ADALN_COL_TILE = 1024
Q_TILE = 512
KV_TILE = 256
BOUNDED_CHUNKS_PER_TRIP = 32
ONLINE_CHUNKS_PER_TRIP = 4
DFT_CH_TILE = 128
DFT_N2_PITCH = DFT_N2 + 8
DFT_N1_PITCH = DFT_N1 + 8
DFT_WIDEN_UNROLL = 8
DFT_STAGE1_UNROLL = 8
DFT_STAGE2_UNROLL = 16
DFT_GATE_UNROLL = 4


def _params(*semantics):
    return pltpu.CompilerParams(dimension_semantics=semantics,
                                vmem_limit_bytes=VMEM_LIMIT_BYTES)


def _silu(x):
    return x * jax.nn.sigmoid(x)


def _adaln_body(c_ref, w_ref, b_ref, o_ref):
    c_act = _silu(c_ref[...])
    o_ref[0] = jnp.dot(c_act, w_ref[0], preferred_element_type=F32,
                       precision=lax.Precision.HIGHEST) + b_ref[0]


def _adaln(c_pad, ada_w, ada_b):
    depth, d, n = ada_w.shape
    rows = c_pad.shape[0]
    return pl.pallas_call(
        _adaln_body,
        grid=(depth, n // ADALN_COL_TILE),
        in_specs=[
            pl.BlockSpec((rows, d), lambda l, j: (0, 0)),
            pl.BlockSpec((1, d, ADALN_COL_TILE), lambda l, j: (l, 0, j)),
            pl.BlockSpec((1, 1, ADALN_COL_TILE), lambda l, j: (l, 0, j)),
        ],
        out_specs=pl.BlockSpec((1, rows, ADALN_COL_TILE), lambda l, j: (l, 0, j)),
        out_shape=jax.ShapeDtypeStruct((depth, rows, n), F32),
        compiler_params=_params("arbitrary", "arbitrary"),
        name="adaln",
    )(c_pad, ada_w, ada_b.reshape(depth, 1, n))


def _modulated_norm(x_ref, g_ref, sc_ref, sh_ref, rows):
    x = x_ref[rows, :]
    ms = jnp.mean(x * x, axis=-1, keepdims=True)
    a = g_ref[...] * (1.0 + sc_ref[0])
    return (x * lax.rsqrt(ms + EPS) * a + sh_ref[0]).astype(BF16)


def _row_subtiles(rows_per_matmul):
    return [slice(r, r + rows_per_matmul) for r in range(0, ROW_TILE, rows_per_matmul)]


def _attn_inproj_body(x_ref, g_ref, sc_ref, sh_ref, w_ref, gain_ref, cos_ref, sina_ref, sinb_ref,
                      qt_ref, k_ref, vt_ref, sg_ref):
    heads_per_tile = COL_TILE // HEAD_DIM
    n_q_tiles = ATTN_WIDTH // COL_TILE
    n_qk_tiles = (ATTN_WIDTH + KV_WIDTH) // COL_TILE
    for rows in _row_subtiles(ATTN_ROW_SUBTILE):
        h = _modulated_norm(x_ref, g_ref, sc_ref, sh_ref, rows)
        cos = cos_ref[rows, :]
        sin_a = sina_ref[rows, :]
        sin_b = sinb_ref[rows, :]
        for j in range(ATTN_IN_WIDTH // COL_TILE):
            acc = jnp.dot(h, w_ref[:, j * COL_TILE:(j + 1) * COL_TILE],
                          preferred_element_type=F32)
            for hh in range(heads_per_tile):
                xh = acc[:, hh * HEAD_DIM:(hh + 1) * HEAD_DIM]
                if j < n_qk_tiles:
                    col0 = j * COL_TILE + hh * HEAD_DIM
                    ms = jnp.mean(xh * xh, axis=-1, keepdims=True)
                    xn = xh * lax.rsqrt(ms + EPS) * gain_ref[:, col0:col0 + HEAD_DIM]
                    rot = (xn * cos + pltpu.roll(xn, HEAD_DIM - 1, axis=1) * sin_a
                           + pltpu.roll(xn, 1, axis=1) * sin_b)
                    if j < n_q_tiles:
                        qt_ref[0, j * heads_per_tile + hh, :, rows] = rot.T.astype(BF16)
                    else:
                        k_ref[0, hh, rows, :] = rot.astype(BF16)
                elif j == n_qk_tiles:
                    vt_ref[0, hh, :, rows] = xh.T.astype(BF16)
            if j > n_qk_tiles:
                g0 = (j - n_qk_tiles - 1) * COL_TILE
                sg_ref[0, rows, g0:g0 + COL_TILE] = _silu(acc).astype(BF16)


def _resident(shape):
    return pl.BlockSpec(shape, lambda *_: (0,) * len(shape), pipeline_mode=pl.Buffered(1))


def _attn_inproj(x2d, norm_g, scale, shift, w_bf16, gain_vec, cos_il, sin_a, sin_b):
    t, d = x2d.shape
    n = w_bf16.shape[1]
    tpb = SEQ // ROW_TILE
    rope_spec = pl.BlockSpec((ROW_TILE, HEAD_DIM), lambda i: (i % tpb, 0))
    return pl.pallas_call(
        _attn_inproj_body,
        grid=(t // ROW_TILE,),
        in_specs=[
            pl.BlockSpec((ROW_TILE, d), lambda i: (i, 0)),
            _resident((1, d)),
            pl.BlockSpec((1, 1, d), lambda i: (i // tpb, 0, 0)),
            pl.BlockSpec((1, 1, d), lambda i: (i // tpb, 0, 0)),
            _resident((d, n)),
            _resident(gain_vec.shape),
            rope_spec, rope_spec, rope_spec,
        ],
        out_specs=[
            pl.BlockSpec((1, N_HEADS, HEAD_DIM, ROW_TILE), lambda i: (i // tpb, 0, 0, i % tpb)),
            pl.BlockSpec((1, N_KV_HEADS, ROW_TILE, HEAD_DIM), lambda i: (i // tpb, 0, i % tpb, 0)),
            pl.BlockSpec((1, N_KV_HEADS, HEAD_DIM, ROW_TILE), lambda i: (i // tpb, 0, 0, i % tpb)),
            pl.BlockSpec((1, ROW_TILE, ATTN_WIDTH), lambda i: (i // tpb, i % tpb, 0)),
        ],
        out_shape=[
            jax.ShapeDtypeStruct((BATCH, N_HEADS, HEAD_DIM, SEQ), BF16),
            jax.ShapeDtypeStruct((BATCH, N_KV_HEADS, SEQ, HEAD_DIM), BF16),
            jax.ShapeDtypeStruct((BATCH, N_KV_HEADS, HEAD_DIM, SEQ), BF16),
            jax.ShapeDtypeStruct((BATCH, SEQ, ATTN_WIDTH), BF16),
        ],
        compiler_params=_params("arbitrary"),
        name="attn_inproj",
    )(x2d, norm_g, scale, shift, w_bf16, gain_vec, cos_il, sin_a, sin_b)


def _attn_body(bounded_ref, qt_ref, k_ref, vt_ref, sg_ref, o_ref, qt_scr, m_scr, l_scr, acc_scr,
               st_scr):
    n_kv_steps = SEQ // KV_TILE

    for hh in range(GQA_GROUP):
        qt_scr[:, hh * Q_TILE:(hh + 1) * Q_TILE] = qt_ref[0, hh]
    m_scr[...] = jnp.full(m_scr.shape, -jnp.inf, F32)
    l_scr[...] = jnp.zeros(l_scr.shape, F32)
    acc_scr[...] = jnp.zeros(acc_scr.shape, F32)

    def chunk(c):
        start = c * KV_TILE if isinstance(c, int) else pl.multiple_of(c * KV_TILE, KV_TILE)
        return pl.ds(start, KV_TILE)

    def scores(c):
        return jnp.dot(k_ref[0, 0, chunk(c), :], qt_scr[...], preferred_element_type=F32)

    def weighted_values(p, c):
        return jnp.dot(vt_ref[0, 0, :, chunk(c)], p.astype(BF16), preferred_element_type=F32)

    def consume_online(st, c):
        m_old = m_scr[...]
        m_new = jnp.maximum(m_old, jnp.max(st, axis=0, keepdims=True))
        alpha = jnp.exp2(m_old - m_new)
        p = jnp.exp2(st - m_new)
        l_scr[0:1] = alpha * l_scr[0:1] + jnp.sum(p, axis=0, keepdims=True)
        acc_scr[...] = alpha * acc_scr[...] + weighted_values(p, c)
        m_scr[...] = m_new

    def consume_bounded(st, c):
        p = jnp.exp2(st)
        l_scr[...] += jnp.sum(p.reshape(KV_TILE // 8, 8, p.shape[1]), axis=0)
        acc_scr[...] += weighted_values(p, c)

    def run(consume, chunks_per_trip):
        if chunks_per_trip == n_kv_steps:
            st = scores(0)
            for j in range(n_kv_steps):
                st_next = scores(j + 1) if j + 1 < n_kv_steps else None
                consume(st, j)
                st = st_next
            return

        st_scr[...] = scores(0)

        def step(i, carry):
            first = i * chunks_per_trip
            st = st_scr[...]
            for j in range(chunks_per_trip):
                nxt = jnp.minimum(first + j + 1, n_kv_steps - 1)
                st_next = scores(nxt)
                consume(st, first + j)
                st = st_next
            st_scr[...] = st
            return carry

        lax.fori_loop(0, n_kv_steps // chunks_per_trip, step, 0)

    @pl.when(bounded_ref[0] != 0)
    def _():
        run(consume_bounded, BOUNDED_CHUNKS_PER_TRIP)

    @pl.when(bounded_ref[0] == 0)
    def _():
        run(consume_online, ONLINE_CHUNKS_PER_TRIP)

    ot = acc_scr[...] / jnp.sum(l_scr[...], axis=0, keepdims=True)
    for hh in range(GQA_GROUP):
        sl = slice(hh * HEAD_DIM, (hh + 1) * HEAD_DIM)
        o = ot[:, hh * Q_TILE:(hh + 1) * Q_TILE].T
        o_ref[0, :, sl] = (o * sg_ref[0, :, sl].astype(F32)).astype(BF16)


def _attention(bounded, qt, k, vt, sg):
    b, _, _, s = qt.shape
    group_w = GQA_GROUP * HEAD_DIM
    nq = GQA_GROUP * Q_TILE
    return pl.pallas_call(
        _attn_body,
        grid=(b, N_KV_HEADS, s // Q_TILE),
        in_specs=[
            pl.BlockSpec(memory_space=pltpu.SMEM),
            pl.BlockSpec((1, GQA_GROUP, HEAD_DIM, Q_TILE), lambda bi, kh, qi: (bi, kh, 0, qi)),
            pl.BlockSpec((1, 1, s, HEAD_DIM), lambda bi, kh, qi: (bi, kh, 0, 0)),
            pl.BlockSpec((1, 1, HEAD_DIM, s), lambda bi, kh, qi: (bi, kh, 0, 0)),
            pl.BlockSpec((1, Q_TILE, group_w), lambda bi, kh, qi: (bi, qi, kh)),
        ],
        out_specs=pl.BlockSpec((1, Q_TILE, group_w), lambda bi, kh, qi: (bi, qi, kh)),
        out_shape=jax.ShapeDtypeStruct((b, s, ATTN_WIDTH), BF16),
        scratch_shapes=[
            pltpu.VMEM((HEAD_DIM, nq), BF16),
            pltpu.VMEM((1, nq), F32),
            pltpu.VMEM((8, nq), F32),
            pltpu.VMEM((HEAD_DIM, nq), F32),
            pltpu.VMEM((KV_TILE, nq), F32),
        ],
        compiler_params=_params("arbitrary", "arbitrary", "arbitrary"),
        name="attention",
    )(bounded, qt, k, vt, sg)


def _outproj_body(a_ref, w_ref, x_ref, gm_ref, o_ref):
    y = jnp.dot(a_ref[...], w_ref[...], preferred_element_type=F32)
    o_ref[...] = x_ref[...] + gm_ref[0] * y


def _outproj_residual(a2d, w_bf16, x2d, gate_mod):
    t, d = x2d.shape
    tiles_per_batch = SEQ // ROW_TILE
    return pl.pallas_call(
        _outproj_body,
        grid=(t // ROW_TILE,),
        in_specs=[
            pl.BlockSpec((ROW_TILE, a2d.shape[1]), lambda i: (i, 0)),
            pl.BlockSpec(w_bf16.shape, lambda i: (0, 0)),
            pl.BlockSpec((ROW_TILE, d), lambda i: (i, 0)),
            pl.BlockSpec((1, 1, d), lambda i: (i // tiles_per_batch, 0, 0)),
        ],
        out_specs=pl.BlockSpec((ROW_TILE, d), lambda i: (i, 0)),
        out_shape=jax.ShapeDtypeStruct((t, d), F32),
        compiler_params=_params("arbitrary"),
        name="attn_outproj",
    )(a2d, w_bf16, x2d, gate_mod)


def _fourier_inproj_body(x_ref, g_ref, sc_ref, sh_ref, w_ref, cs_ref, z_ref, sg_ref):
    n_u_tiles = D_MODEL // COL_TILE
    for rows in _row_subtiles(FOURIER_ROW_SUBTILE):
        h = _modulated_norm(x_ref, g_ref, sc_ref, sh_ref, rows)
        for j in range(2 * n_u_tiles):
            acc = jnp.dot(h, w_ref[:, j * COL_TILE:(j + 1) * COL_TILE],
                          preferred_element_type=F32)
            if j < n_u_tiles:
                for gg in range(COL_TILE // FOURIER_GROUP_W):
                    sl = slice(j * COL_TILE + gg * FOURIER_GROUP_W,
                               j * COL_TILE + (gg + 1) * FOURIER_GROUP_W)
                    u = acc[:, gg * FOURIER_GROUP_W:(gg + 1) * FOURIER_GROUP_W].astype(BF16)
                    z = jnp.dot(u, cs_ref[...], preferred_element_type=F32)
                    z_ref[0, 0, rows, sl] = z[:, :FOURIER_GROUP_W].astype(BF16)
                    z_ref[0, 1, rows, sl] = z[:, FOURIER_GROUP_W:].astype(BF16)
            else:
                cols = slice((j - n_u_tiles) * COL_TILE, (j - n_u_tiles + 1) * COL_TILE)
                sg_ref[0, rows, cols] = _silu(acc).astype(BF16)


def _fourier_inproj(x2d, norm_g, scale, shift, w_bf16, cs):
    t, d = x2d.shape
    tiles_per_batch = SEQ // ROW_TILE
    return pl.pallas_call(
        _fourier_inproj_body,
        grid=(t // ROW_TILE,),
        in_specs=[
            pl.BlockSpec((ROW_TILE, d), lambda i: (i, 0)),
            _resident((1, d)),
            pl.BlockSpec((1, 1, d), lambda i: (i // tiles_per_batch, 0, 0)),
            pl.BlockSpec((1, 1, d), lambda i: (i // tiles_per_batch, 0, 0)),
            _resident(w_bf16.shape),
            _resident(cs.shape),
        ],
        out_specs=[
            pl.BlockSpec((1, 2, ROW_TILE, D_MODEL),
                         lambda i: (i // tiles_per_batch, 0, i % tiles_per_batch, 0)),
            pl.BlockSpec((1, ROW_TILE, D_MODEL),
                         lambda i: (i // tiles_per_batch, i % tiles_per_batch, 0)),
        ],
        out_shape=[jax.ShapeDtypeStruct((BATCH, 2, SEQ, D_MODEL), BF16),
                   jax.ShapeDtypeStruct((BATCH, SEQ, D_MODEL), BF16)],
        compiler_params=_params("arbitrary"),
        name="fourier_inproj",
    )(x2d, norm_g, scale, shift, w_bf16, cs)


def _seq_dft_body(z_ref, sg_ref, m_ref, t_ref, o_ref, zf_scr, a_scr, f_scr):
    c = DFT_CH_TILE
    im_rows = DFT_N1 * DFT_N2_PITCH

    def widen(i, carry):
        for u in range(DFT_WIDEN_UNROLL):
            n1 = i * DFT_WIDEN_UNROLL + u
            src = pl.multiple_of(n1 * DFT_N2, DFT_N2)
            dst = pl.multiple_of(n1 * DFT_N2_PITCH, 8)
            zf_scr[pl.ds(dst, DFT_N2), :] = z_ref[0, 0, pl.ds(src, DFT_N2), :].astype(F32)
            zf_scr[pl.ds(im_rows + dst, DFT_N2), :] = (
                z_ref[0, 1, pl.ds(src, DFT_N2), :].astype(F32))
        return carry

    lax.fori_loop(0, DFT_N1 // DFT_WIDEN_UNROLL, widen, 0)

    def stage1(i, carry):
        for u in range(DFT_STAGE1_UNROLL):
            n2 = (i * DFT_STAGE1_UNROLL + u) * 2
            halves = []
            for dn in range(2):
                xr = zf_scr[pl.ds(n2 + dn, DFT_N1, stride=DFT_N2_PITCH), :]
                xi = zf_scr[pl.ds(im_rows + n2 + dn, DFT_N1, stride=DFT_N2_PITCH), :]
                halves.append(jnp.concatenate([xr, xi], axis=0))
            x = jnp.concatenate(halves, axis=1).astype(BF16)
            a = jnp.dot(m_ref[...], x, preferred_element_type=F32)
            for dn in range(2):
                a_scr[pl.ds(n2 + dn, 2 * DFT_N1, stride=DFT_N2_PITCH), :] = (
                    a[:, dn * c:(dn + 1) * c])
        return carry

    lax.fori_loop(0, DFT_N2 // (2 * DFT_STAGE1_UNROLL), stage1, 0)

    def stage2(i, carry):
        for u in range(DFT_STAGE2_UNROLL):
            k1 = i * DFT_STAGE2_UNROLL + u
            r = pl.multiple_of(k1 * 2 * DFT_N2_PITCH, 8)
            rows = jnp.concatenate([a_scr[pl.ds(r, DFT_N2), :],
                                    a_scr[pl.ds(r + DFT_N2_PITCH, DFT_N2), :]],
                                   axis=0).astype(BF16)
            f = jnp.dot(t_ref[k1], rows, preferred_element_type=F32)
            f_scr[pl.ds(k1, DFT_N2, stride=DFT_N1_PITCH), :] = f
        return carry

    lax.fori_loop(0, DFT_N1 // DFT_STAGE2_UNROLL, stage2, 0)

    def gate(i, carry):
        for u in range(DFT_GATE_UNROLL):
            k2 = i * DFT_GATE_UNROLL + u
            src = pl.multiple_of(k2 * DFT_N1_PITCH, 8)
            dst = pl.multiple_of(k2 * DFT_N1, DFT_N1)
            o_ref[0, pl.ds(dst, DFT_N1), :] = (
                f_scr[pl.ds(src, DFT_N1), :]
                * sg_ref[0, pl.ds(dst, DFT_N1), :].astype(F32)).astype(BF16)
        return carry

    lax.fori_loop(0, DFT_N2 // DFT_GATE_UNROLL, gate, 0)


def _seq_dft_gate(z, sg, m1, t12):
    b, _, s, d = z.shape
    c = DFT_CH_TILE
    return pl.pallas_call(
        _seq_dft_body,
        grid=(b, d // c),
        in_specs=[
            pl.BlockSpec((1, 2, s, c), lambda bi, ci: (bi, 0, 0, ci)),
            pl.BlockSpec((1, s, c), lambda bi, ci: (bi, 0, ci)),
            _resident(m1.shape),
            _resident(t12.shape),
        ],
        out_specs=pl.BlockSpec((1, s, c), lambda bi, ci: (bi, 0, ci)),
        out_shape=jax.ShapeDtypeStruct((b, s, d), BF16),
        scratch_shapes=[
            pltpu.VMEM((2 * DFT_N1 * DFT_N2_PITCH, c), F32),
            pltpu.VMEM((2 * DFT_N1 * DFT_N2_PITCH, c), F32),
            pltpu.VMEM((DFT_N2 * DFT_N1_PITCH, c), F32),
        ],
        compiler_params=_params("arbitrary", "arbitrary"),
        name="seq_dft_gate",
    )(z, sg, m1, t12)


def _final_body(a_ref, w_ref, x_ref, gm_ref, fg_ref, o_ref):
    y = jnp.dot(a_ref[...], w_ref[...], preferred_element_type=F32)
    x2 = x_ref[...] + gm_ref[0] * y
    ms = jnp.mean(x2 * x2, axis=-1, keepdims=True)
    o_ref[...] = x2 * lax.rsqrt(ms + EPS) * fg_ref[...]


def _fourier_outproj_final(a2d, w_bf16, x2d, gate_mod, final_g):
    t, d = x2d.shape
    tiles_per_batch = SEQ // ROW_TILE
    return pl.pallas_call(
        _final_body,
        grid=(t // ROW_TILE,),
        in_specs=[
            pl.BlockSpec((ROW_TILE, a2d.shape[1]), lambda i: (i, 0)),
            _resident(w_bf16.shape),
            pl.BlockSpec((ROW_TILE, d), lambda i: (i, 0)),
            pl.BlockSpec((1, 1, d), lambda i: (i // tiles_per_batch, 0, 0)),
            _resident((1, d)),
        ],
        out_specs=pl.BlockSpec((ROW_TILE, d), lambda i: (i, 0)),
        out_shape=jax.ShapeDtypeStruct((t, d), F32),
        compiler_params=_params("arbitrary"),
        name="fourier_outproj_final",
    )(a2d, w_bf16, x2d, gate_mod, final_g)


def _rope_tables():
    rows = SEQ // GRID_W
    inv_freq = ROPE_THETA ** (-jnp.arange(0, ROPE_AXIS_DIM, 2, dtype=F32) / ROPE_AXIS_DIM)
    n_freq = inv_freq.shape[0]
    pair = np.arange(HEAD_DIM) // 2
    lane_freq = inv_freq[pair % n_freq][None, :]
    uses_row = jnp.asarray(pair < n_freq)[None, None, :]
    even_lane = jnp.asarray(np.arange(HEAD_DIM) % 2 == 0)[None, None, :]
    row_ang = jnp.arange(rows).astype(F32)[:, None] * lane_freq
    col_ang = jnp.arange(GRID_W).astype(F32)[:, None] * lane_freq

    def per_position(fn):
        return jnp.where(uses_row, fn(row_ang)[:, None, :], fn(col_ang)[None, :, :])

    cos, sin = per_position(jnp.cos), per_position(jnp.sin)
    flat = lambda a: a.reshape(SEQ, HEAD_DIM)
    return (flat(cos), flat(jnp.where(even_lane, -sin, 0.0)), flat(jnp.where(even_lane, 0.0, sin)))


def _dft_tables():
    w = np.arange(FOURIER_GROUP_W)
    ang_w = 2.0 * np.pi * (np.outer(w, w) % FOURIER_GROUP_W) / FOURIER_GROUP_W
    cs = np.concatenate([np.cos(ang_w), -np.sin(ang_w)], axis=1) * 2.0 ** -4

    n1 = np.arange(DFT_N1)
    ang1 = 2.0 * np.pi * (np.outer(n1, n1) % DFT_N1) / DFT_N1
    c1, s1 = np.cos(ang1), np.sin(ang1)
    m1 = np.stack([np.concatenate([c1, s1], axis=1),
                   np.concatenate([-s1, c1], axis=1)], axis=1).reshape(2 * DFT_N1, 2 * DFT_N1)
    m1 = m1 * 2.0 ** -3

    k = n1[:, None] + DFT_N1 * np.arange(DFT_N2)[None, :]
    n2 = np.arange(DFT_N2)
    ang2 = 2.0 * np.pi * ((k[:, :, None] * n2[None, None, :]) % SEQ) / SEQ
    t12 = np.concatenate([np.cos(ang2), np.sin(ang2)], axis=2) * 2.0 ** -3.5
    as_bf16 = lambda a: jnp.asarray(a, dtype=F32).astype(BF16)
    return as_bf16(cs), as_bf16(m1), as_bf16(t12)


def kernel(x, c, norm_g, ada_w, ada_b, attn_w_in, attn_q_gain, attn_k_gain, attn_w_out,
           fourier_w_in, fourier_w_out, final_g):
    b, s, d = x.shape
    t = b * s

    c_pad = jnp.zeros((8, d), F32).at[:b].set(c)
    mod = _adaln(c_pad, ada_w, ada_b)[:, :b]
    shift = mod[:, :, None, :d]
    scale = mod[:, :, None, d:2 * d]
    gate = mod[:, :, None, 2 * d:]

    q_gain = attn_q_gain[0] * (math.log2(math.e) / math.sqrt(HEAD_DIM))
    k_gain = attn_k_gain[0]
    gain_vec = jnp.concatenate([jnp.tile(q_gain, N_HEADS), jnp.tile(k_gain, N_KV_HEADS)])[None, :]
    cos_il, sin_a, sin_b = _rope_tables()

    x2d = x.reshape(t, d)
    qt, k, vt, sg0 = _attn_inproj(x2d, norm_g[0][None, :], scale[0], shift[0],
                                  attn_w_in[0].astype(BF16), gain_vec, cos_il, sin_a, sin_b)
    score_bound = (HEAD_DIM * BF16_NORM_SLACK * jnp.max(jnp.abs(q_gain)) * jnp.max(jnp.abs(k_gain)))
    bounded = (score_bound <= SCORE_BOUND_LOG2).astype(jnp.int32).reshape(1)
    og = _attention(bounded, qt, k, vt, sg0)
    x1 = _outproj_residual(og.reshape(t, ATTN_WIDTH), attn_w_out[0].astype(BF16), x2d, gate[0])

    cs, m1, t12 = _dft_tables()
    z, sg = _fourier_inproj(x1, norm_g[1][None, :], scale[1], shift[1],
                            fourier_w_in[0].astype(BF16), cs)
    fgate = _seq_dft_gate(z, sg, m1, t12)
    out = _fourier_outproj_final(fgate.reshape(t, d), fourier_w_out[0].astype(BF16), x1,
                                 gate[1], final_g[None, :])
    return out.reshape(b, s, d)
```

```python
import math

import numpy as np
import jax
import jax.numpy as jnp
from jax import lax
from jax.experimental import pallas as pl
from jax.experimental.pallas import tpu as pltpu

D_MODEL = 2048
BATCH = 4
SEQ = 8192
GRID_W = 64
HEAD_DIM = 128
N_HEADS = 16
N_KV_HEADS = 4
GQA_GROUP = N_HEADS // N_KV_HEADS
ATTN_WIDTH = N_HEADS * HEAD_DIM
KV_WIDTH = N_KV_HEADS * HEAD_DIM
ATTN_IN_WIDTH = 2 * ATTN_WIDTH + 2 * KV_WIDTH
ROPE_AXIS_DIM = HEAD_DIM // 2
ROPE_THETA = 10000.0
FOURIER_GROUPS = 8
FOURIER_GROUP_W = D_MODEL // FOURIER_GROUPS
EPS = 1e-6

DFT_N1 = 128
DFT_N2 = SEQ // DFT_N1

F32 = jnp.float32
BF16 = jnp.bfloat16

SCORE_BOUND_LOG2 = 64.0
BF16_NORM_SLACK = 1.01

VMEM_LIMIT_BYTES = 56 * 1024 * 1024

ROW_TILE = 512
ATTN_ROW_SUBTILE = 256
FOURIER_ROW_SUBTILE = 512
COL_TILE = 512
ADALN_COL_TILE = 1024
Q_TILE = ROW_TILE
KV_TILE = 256
BOUNDED_CHUNKS_PER_TRIP = 32
ONLINE_CHUNKS_PER_TRIP = 4
DFT_CH_TILE = 128
DFT_N2_PITCH = DFT_N2 + 8
DFT_N1_PITCH = DFT_N1 + 8
DFT_WIDEN_UNROLL = 8
DFT_STAGE1_UNROLL = 8
DFT_STAGE2_UNROLL = 16
DFT_GATE_UNROLL = 4


def _params(*semantics):
    return pltpu.CompilerParams(dimension_semantics=semantics,
                                vmem_limit_bytes=VMEM_LIMIT_BYTES)


def _silu(x):
    return x * jax.nn.sigmoid(x)


def _adaln_body(c_ref, w_ref, b_ref, o_ref):
    c_act = _silu(c_ref[...])
    o_ref[0] = jnp.dot(c_act, w_ref[0], preferred_element_type=F32,
                       precision=lax.Precision.HIGHEST) + b_ref[0]


def _adaln(c_pad, ada_w, ada_b):
    depth, d, n = ada_w.shape
    rows = c_pad.shape[0]
    return pl.pallas_call(
        _adaln_body,
        grid=(depth, n // ADALN_COL_TILE),
        in_specs=[
            pl.BlockSpec((rows, d), lambda l, j: (0, 0)),
            pl.BlockSpec((1, d, ADALN_COL_TILE), lambda l, j: (l, 0, j)),
            pl.BlockSpec((1, 1, ADALN_COL_TILE), lambda l, j: (l, 0, j)),
        ],
        out_specs=pl.BlockSpec((1, rows, ADALN_COL_TILE), lambda l, j: (l, 0, j)),
        out_shape=jax.ShapeDtypeStruct((depth, rows, n), F32),
        compiler_params=_params("arbitrary", "arbitrary"),
        name="adaln",
    )(c_pad, ada_w, ada_b.reshape(depth, 1, n))


def _modulated_norm(x_ref, g_ref, sc_ref, sh_ref, rows):
    x = x_ref[rows, :]
    ms = jnp.mean(x * x, axis=-1, keepdims=True)
    a = g_ref[...] * (1.0 + sc_ref[0])
    return (x * lax.rsqrt(ms + EPS) * a + sh_ref[0]).astype(BF16)


def _row_subtiles(rows_per_matmul):
    return [slice(r, r + rows_per_matmul) for r in range(0, ROW_TILE, rows_per_matmul)]


def _attn_inproj_body(x_ref, g_ref, sc_ref, sh_ref, w_ref, gain_ref, cos_ref, sina_ref, sinb_ref,
                      qt_ref, k_ref, vt_ref, sg_ref):
    heads_per_tile = COL_TILE // HEAD_DIM
    n_q_tiles = ATTN_WIDTH // COL_TILE
    n_qk_tiles = (ATTN_WIDTH + KV_WIDTH) // COL_TILE
    for rows in _row_subtiles(ATTN_ROW_SUBTILE):
        h = _modulated_norm(x_ref, g_ref, sc_ref, sh_ref, rows)
        cos = cos_ref[rows, :]
        sin_a = sina_ref[rows, :]
        sin_b = sinb_ref[rows, :]
        for j in range(ATTN_IN_WIDTH // COL_TILE):
            acc = jnp.dot(h, w_ref[:, j * COL_TILE:(j + 1) * COL_TILE],
                          preferred_element_type=F32)
            for hh in range(heads_per_tile):
                xh = acc[:, hh * HEAD_DIM:(hh + 1) * HEAD_DIM]
                if j < n_qk_tiles:
                    col0 = j * COL_TILE + hh * HEAD_DIM
                    ms = jnp.mean(xh * xh, axis=-1, keepdims=True)
                    xn = xh * lax.rsqrt(ms + EPS) * gain_ref[:, col0:col0 + HEAD_DIM]
                    rot = (xn * cos + pltpu.roll(xn, HEAD_DIM - 1, axis=1) * sin_a
                           + pltpu.roll(xn, 1, axis=1) * sin_b)
                    if j < n_q_tiles:
                        lanes = slice(hh * ROW_TILE + rows.start, hh * ROW_TILE + rows.stop)
                        qt_ref[0, j, 0, :, lanes] = rot.T.astype(BF16)
                    else:
                        k_ref[0, hh, rows, :] = rot.astype(BF16)
                elif j == n_qk_tiles:
                    vt_ref[0, hh, :, rows] = xh.T.astype(BF16)
            if j > n_qk_tiles:
                g0 = (j - n_qk_tiles - 1) * COL_TILE
                sg_ref[0, rows, g0:g0 + COL_TILE] = _silu(acc).astype(BF16)


def _resident(shape):
    return pl.BlockSpec(shape, lambda *_: (0,) * len(shape), pipeline_mode=pl.Buffered(1))


def _attn_inproj(x2d, norm_g, scale, shift, w_bf16, gain_vec, cos_il, sin_a, sin_b):
    t, d = x2d.shape
    n = w_bf16.shape[1]
    tpb = SEQ // ROW_TILE
    rope_spec = pl.BlockSpec((ROW_TILE, HEAD_DIM), lambda i: (i % tpb, 0))
    return pl.pallas_call(
        _attn_inproj_body,
        grid=(t // ROW_TILE,),
        in_specs=[
            pl.BlockSpec((ROW_TILE, d), lambda i: (i, 0)),
            _resident((1, d)),
            pl.BlockSpec((1, 1, d), lambda i: (i // tpb, 0, 0)),
            pl.BlockSpec((1, 1, d), lambda i: (i // tpb, 0, 0)),
            _resident((d, n)),
            _resident(gain_vec.shape),
            rope_spec, rope_spec, rope_spec,
        ],
        out_specs=[
            pl.BlockSpec((1, N_KV_HEADS, 1, HEAD_DIM, GQA_GROUP * ROW_TILE),
                         lambda i: (i // tpb, 0, i % tpb, 0, 0)),
            pl.BlockSpec((1, N_KV_HEADS, ROW_TILE, HEAD_DIM), lambda i: (i // tpb, 0, i % tpb, 0)),
            pl.BlockSpec((1, N_KV_HEADS, HEAD_DIM, ROW_TILE), lambda i: (i // tpb, 0, 0, i % tpb)),
            pl.BlockSpec((1, ROW_TILE, ATTN_WIDTH), lambda i: (i // tpb, i % tpb, 0)),
        ],
        out_shape=[
            jax.ShapeDtypeStruct((BATCH, N_KV_HEADS, tpb, HEAD_DIM, GQA_GROUP * ROW_TILE), BF16),
            jax.ShapeDtypeStruct((BATCH, N_KV_HEADS, SEQ, HEAD_DIM), BF16),
            jax.ShapeDtypeStruct((BATCH, N_KV_HEADS, HEAD_DIM, SEQ), BF16),
            jax.ShapeDtypeStruct((BATCH, SEQ, ATTN_WIDTH), BF16),
        ],
        compiler_params=_params("arbitrary"),
        name="attn_inproj",
    )(x2d, norm_g, scale, shift, w_bf16, gain_vec, cos_il, sin_a, sin_b)


def _attn_body(bounded_ref, qt_ref, k_ref, vt_ref, sg_ref, o_ref, m_scr, l_scr, acc_scr, st_scr):
    n_kv_steps = SEQ // KV_TILE
    qt = qt_ref.at[0, 0, 0]

    def chunk(c):
        start = c * KV_TILE if isinstance(c, int) else pl.multiple_of(c * KV_TILE, KV_TILE)
        return pl.ds(start, KV_TILE)

    def scores(c):
        return jnp.dot(k_ref[0, 0, chunk(c), :], qt[...], preferred_element_type=F32)

    def weighted_values(p, c):
        return jnp.dot(vt_ref[0, 0, :, chunk(c)], p.astype(BF16), preferred_element_type=F32)

    def consume_online(st, c, first):
        del first
        m_old = m_scr[...]
        m_new = jnp.maximum(m_old, jnp.max(st, axis=0, keepdims=True))
        alpha = jnp.exp2(m_old - m_new)
        p = jnp.exp2(st - m_new)
        l_scr[0:1] = alpha * l_scr[0:1] + jnp.sum(p, axis=0, keepdims=True)
        acc_scr[...] = alpha * acc_scr[...] + weighted_values(p, c)
        m_scr[...] = m_new

    def consume_bounded(st, c, first):
        p = jnp.exp2(st)
        part = jnp.sum(p.reshape(KV_TILE // 8, 8, p.shape[1]), axis=0)
        pv = weighted_values(p, c)
        if first:
            l_scr[...] = part
            acc_scr[...] = pv
        else:
            l_scr[...] += part
            acc_scr[...] += pv

    def run(consume, chunks_per_trip):
        if chunks_per_trip == n_kv_steps:
            st = scores(0)
            for j in range(n_kv_steps):
                st_next = scores(j + 1) if j + 1 < n_kv_steps else None
                consume(st, j, j == 0)
                st = st_next
            return

        m_scr[...] = jnp.full(m_scr.shape, -jnp.inf, F32)
        l_scr[...] = jnp.zeros(l_scr.shape, F32)
        acc_scr[...] = jnp.zeros(acc_scr.shape, F32)
        st_scr[...] = scores(0)

        def step(i, carry):
            first = i * chunks_per_trip
            st = st_scr[...]
            for j in range(chunks_per_trip):
                nxt = jnp.minimum(first + j + 1, n_kv_steps - 1)
                st_next = scores(nxt)
                consume(st, first + j, False)
                st = st_next
            st_scr[...] = st
            return carry

        lax.fori_loop(0, n_kv_steps // chunks_per_trip, step, 0)

    @pl.when(bounded_ref[0] != 0)
    def _():
        run(consume_bounded, BOUNDED_CHUNKS_PER_TRIP)

    @pl.when(bounded_ref[0] == 0)
    def _():
        run(consume_online, ONLINE_CHUNKS_PER_TRIP)

    ot = acc_scr[...] * (1.0 / jnp.sum(l_scr[...], axis=0, keepdims=True))
    for hh in range(GQA_GROUP):
        sl = slice(hh * HEAD_DIM, (hh + 1) * HEAD_DIM)
        o = ot[:, hh * Q_TILE:(hh + 1) * Q_TILE].T
        o_ref[0, :, sl] = (o * sg_ref[0, :, sl].astype(F32)).astype(BF16)


def _attention(bounded, qt, k, vt, sg):
    b, _, s, _ = k.shape
    group_w = GQA_GROUP * HEAD_DIM
    nq = GQA_GROUP * Q_TILE
    return pl.pallas_call(
        _attn_body,
        grid=(b, N_KV_HEADS, s // Q_TILE),
        in_specs=[
            pl.BlockSpec(memory_space=pltpu.SMEM),
            pl.BlockSpec((1, 1, 1, HEAD_DIM, nq), lambda bi, kh, qi: (bi, kh, qi, 0, 0)),
            pl.BlockSpec((1, 1, s, HEAD_DIM), lambda bi, kh, qi: (bi, kh, 0, 0)),
            pl.BlockSpec((1, 1, HEAD_DIM, s), lambda bi, kh, qi: (bi, kh, 0, 0)),
            pl.BlockSpec((1, Q_TILE, group_w), lambda bi, kh, qi: (bi, qi, kh)),
        ],
        out_specs=pl.BlockSpec((1, Q_TILE, group_w), lambda bi, kh, qi: (bi, qi, kh)),
        out_shape=jax.ShapeDtypeStruct((b, s, ATTN_WIDTH), BF16),
        scratch_shapes=[
            pltpu.VMEM((1, nq), F32),
            pltpu.VMEM((8, nq), F32),
            pltpu.VMEM((HEAD_DIM, nq), F32),
            pltpu.VMEM((KV_TILE, nq), F32),
        ],
        compiler_params=_params("arbitrary", "arbitrary", "arbitrary"),
        name="attention",
    )(bounded, qt, k, vt, sg)


def _outproj_body(a_ref, w_ref, x_ref, gm_ref, o_ref):
    y = jnp.dot(a_ref[...], w_ref[...], preferred_element_type=F32)
    o_ref[...] = x_ref[...] + gm_ref[0] * y


def _outproj_residual(a2d, w_bf16, x2d, gate_mod):
    t, d = x2d.shape
    tiles_per_batch = SEQ // ROW_TILE
    return pl.pallas_call(
        _outproj_body,
        grid=(t // ROW_TILE,),
        in_specs=[
            pl.BlockSpec((ROW_TILE, a2d.shape[1]), lambda i: (i, 0)),
            pl.BlockSpec(w_bf16.shape, lambda i: (0, 0)),
            pl.BlockSpec((ROW_TILE, d), lambda i: (i, 0)),
            pl.BlockSpec((1, 1, d), lambda i: (i // tiles_per_batch, 0, 0)),
        ],
        out_specs=pl.BlockSpec((ROW_TILE, d), lambda i: (i, 0)),
        out_shape=jax.ShapeDtypeStruct((t, d), F32),
        compiler_params=_params("arbitrary"),
        name="attn_outproj",
    )(a2d, w_bf16, x2d, gate_mod)


def _fourier_inproj_body(x_ref, g_ref, sc_ref, sh_ref, w_ref, cs_ref, z_ref, sg_ref):
    n_u_tiles = D_MODEL // COL_TILE
    for rows in _row_subtiles(FOURIER_ROW_SUBTILE):
        h = _modulated_norm(x_ref, g_ref, sc_ref, sh_ref, rows)
        for j in range(2 * n_u_tiles):
            acc = jnp.dot(h, w_ref[:, j * COL_TILE:(j + 1) * COL_TILE],
                          preferred_element_type=F32)
            if j < n_u_tiles:
                for gg in range(COL_TILE // FOURIER_GROUP_W):
                    sl = slice(j * COL_TILE + gg * FOURIER_GROUP_W,
                               j * COL_TILE + (gg + 1) * FOURIER_GROUP_W)
                    u = acc[:, gg * FOURIER_GROUP_W:(gg + 1) * FOURIER_GROUP_W].astype(BF16)
                    z = jnp.dot(u, cs_ref[...], preferred_element_type=F32)
                    z_ref[0, 0, rows, sl] = z[:, :FOURIER_GROUP_W].astype(BF16)
                    z_ref[0, 1, rows, sl] = z[:, FOURIER_GROUP_W:].astype(BF16)
            else:
                cols = slice((j - n_u_tiles) * COL_TILE, (j - n_u_tiles + 1) * COL_TILE)
                sg_ref[0, rows, cols] = _silu(acc).astype(BF16)


def _fourier_inproj(x2d, norm_g, scale, shift, w_bf16, cs):
    t, d = x2d.shape
    tiles_per_batch = SEQ // ROW_TILE
    return pl.pallas_call(
        _fourier_inproj_body,
        grid=(t // ROW_TILE,),
        in_specs=[
            pl.BlockSpec((ROW_TILE, d), lambda i: (i, 0)),
            _resident((1, d)),
            pl.BlockSpec((1, 1, d), lambda i: (i // tiles_per_batch, 0, 0)),
            pl.BlockSpec((1, 1, d), lambda i: (i // tiles_per_batch, 0, 0)),
            _resident(w_bf16.shape),
            _resident(cs.shape),
        ],
        out_specs=[
            pl.BlockSpec((1, 2, ROW_TILE, D_MODEL),
                         lambda i: (i // tiles_per_batch, 0, i % tiles_per_batch, 0)),
            pl.BlockSpec((1, ROW_TILE, D_MODEL),
                         lambda i: (i // tiles_per_batch, i % tiles_per_batch, 0)),
        ],
        out_shape=[jax.ShapeDtypeStruct((BATCH, 2, SEQ, D_MODEL), BF16),
                   jax.ShapeDtypeStruct((BATCH, SEQ, D_MODEL), BF16)],
        compiler_params=_params("arbitrary"),
        name="fourier_inproj",
    )(x2d, norm_g, scale, shift, w_bf16, cs)


def _seq_dft_body(z_ref, sg_ref, m_ref, t_ref, o_ref, zf_scr, a_scr, f_scr):
    c = DFT_CH_TILE
    im_rows = DFT_N1 * DFT_N2_PITCH

    def widen(i, carry):
        for u in range(DFT_WIDEN_UNROLL):
            n1 = i * DFT_WIDEN_UNROLL + u
            src = pl.multiple_of(n1 * DFT_N2, DFT_N2)
            dst = pl.multiple_of(n1 * DFT_N2_PITCH, 8)
            zf_scr[pl.ds(dst, DFT_N2), :] = z_ref[0, 0, pl.ds(src, DFT_N2), :].astype(F32)
            zf_scr[pl.ds(im_rows + dst, DFT_N2), :] = (
                z_ref[0, 1, pl.ds(src, DFT_N2), :].astype(F32))
        return carry

    lax.fori_loop(0, DFT_N1 // DFT_WIDEN_UNROLL, widen, 0)

    def stage1(i, carry):
        for u in range(DFT_STAGE1_UNROLL):
            n2 = (i * DFT_STAGE1_UNROLL + u) * 2
            halves = []
            for dn in range(2):
                xr = zf_scr[pl.ds(n2 + dn, DFT_N1, stride=DFT_N2_PITCH), :]
                xi = zf_scr[pl.ds(im_rows + n2 + dn, DFT_N1, stride=DFT_N2_PITCH), :]
                halves.append(jnp.concatenate([xr, xi], axis=0))
            x = jnp.concatenate(halves, axis=1).astype(BF16)
            a = jnp.dot(m_ref[...], x, preferred_element_type=F32)
            for dn in range(2):
                a_scr[pl.ds(n2 + dn, 2 * DFT_N1, stride=DFT_N2_PITCH), :] = (
                    a[:, dn * c:(dn + 1) * c])
        return carry

    lax.fori_loop(0, DFT_N2 // (2 * DFT_STAGE1_UNROLL), stage1, 0)

    def stage2(i, carry):
        for u in range(DFT_STAGE2_UNROLL):
            k1 = i * DFT_STAGE2_UNROLL + u
            r = pl.multiple_of(k1 * 2 * DFT_N2_PITCH, 8)
            rows = jnp.concatenate([a_scr[pl.ds(r, DFT_N2), :],
                                    a_scr[pl.ds(r + DFT_N2_PITCH, DFT_N2), :]],
                                   axis=0).astype(BF16)
            f = jnp.dot(t_ref[k1], rows, preferred_element_type=F32)
            f_scr[pl.ds(k1, DFT_N2, stride=DFT_N1_PITCH), :] = f
        return carry

    lax.fori_loop(0, DFT_N1 // DFT_STAGE2_UNROLL, stage2, 0)

    def gate(i, carry):
        for u in range(DFT_GATE_UNROLL):
            k2 = i * DFT_GATE_UNROLL + u
            src = pl.multiple_of(k2 * DFT_N1_PITCH, 8)
            dst = pl.multiple_of(k2 * DFT_N1, DFT_N1)
            o_ref[0, pl.ds(dst, DFT_N1), :] = (
                f_scr[pl.ds(src, DFT_N1), :]
                * sg_ref[0, pl.ds(dst, DFT_N1), :].astype(F32)).astype(BF16)
        return carry

    lax.fori_loop(0, DFT_N2 // DFT_GATE_UNROLL, gate, 0)


def _seq_dft_gate(z, sg, m1, t12):
    b, _, s, d = z.shape
    c = DFT_CH_TILE
    return pl.pallas_call(
        _seq_dft_body,
        grid=(b, d // c),
        in_specs=[
            pl.BlockSpec((1, 2, s, c), lambda bi, ci: (bi, 0, 0, ci)),
            pl.BlockSpec((1, s, c), lambda bi, ci: (bi, 0, ci)),
            _resident(m1.shape),
            _resident(t12.shape),
        ],
        out_specs=pl.BlockSpec((1, s, c), lambda bi, ci: (bi, 0, ci)),
        out_shape=jax.ShapeDtypeStruct((b, s, d), BF16),
        scratch_shapes=[
            pltpu.VMEM((2 * DFT_N1 * DFT_N2_PITCH, c), F32),
            pltpu.VMEM((2 * DFT_N1 * DFT_N2_PITCH, c), F32),
            pltpu.VMEM((DFT_N2 * DFT_N1_PITCH, c), F32),
        ],
        compiler_params=_params("arbitrary", "arbitrary"),
        name="seq_dft_gate",
    )(z, sg, m1, t12)


def _final_body(a_ref, w_ref, x_ref, gm_ref, fg_ref, o_ref):
    y = jnp.dot(a_ref[...], w_ref[...], preferred_element_type=F32)
    x2 = x_ref[...] + gm_ref[0] * y
    ms = jnp.mean(x2 * x2, axis=-1, keepdims=True)
    o_ref[...] = x2 * lax.rsqrt(ms + EPS) * fg_ref[...]


def _fourier_outproj_final(a2d, w_bf16, x2d, gate_mod, final_g):
    t, d = x2d.shape
    tiles_per_batch = SEQ // ROW_TILE
    return pl.pallas_call(
        _final_body,
        grid=(t // ROW_TILE,),
        in_specs=[
            pl.BlockSpec((ROW_TILE, a2d.shape[1]), lambda i: (i, 0)),
            _resident(w_bf16.shape),
            pl.BlockSpec((ROW_TILE, d), lambda i: (i, 0)),
            pl.BlockSpec((1, 1, d), lambda i: (i // tiles_per_batch, 0, 0)),
            _resident((1, d)),
        ],
        out_specs=pl.BlockSpec((ROW_TILE, d), lambda i: (i, 0)),
        out_shape=jax.ShapeDtypeStruct((t, d), F32),
        compiler_params=_params("arbitrary"),
        name="fourier_outproj_final",
    )(a2d, w_bf16, x2d, gate_mod, final_g)


def _rope_tables():
    rows = SEQ // GRID_W
    inv_freq = ROPE_THETA ** (-jnp.arange(0, ROPE_AXIS_DIM, 2, dtype=F32) / ROPE_AXIS_DIM)
    n_freq = inv_freq.shape[0]
    pair = np.arange(HEAD_DIM) // 2
    lane_freq = inv_freq[pair % n_freq][None, :]
    uses_row = jnp.asarray(pair < n_freq)[None, None, :]
    even_lane = jnp.asarray(np.arange(HEAD_DIM) % 2 == 0)[None, None, :]
    row_ang = jnp.arange(rows).astype(F32)[:, None] * lane_freq
    col_ang = jnp.arange(GRID_W).astype(F32)[:, None] * lane_freq

    def per_position(fn):
        return jnp.where(uses_row, fn(row_ang)[:, None, :], fn(col_ang)[None, :, :])

    cos, sin = per_position(jnp.cos), per_position(jnp.sin)
    flat = lambda a: a.reshape(SEQ, HEAD_DIM)
    return (flat(cos), flat(jnp.where(even_lane, -sin, 0.0)), flat(jnp.where(even_lane, 0.0, sin)))


def _dft_tables():
    w = np.arange(FOURIER_GROUP_W)
    ang_w = 2.0 * np.pi * (np.outer(w, w) % FOURIER_GROUP_W) / FOURIER_GROUP_W
    cs = np.concatenate([np.cos(ang_w), -np.sin(ang_w)], axis=1) * 2.0 ** -4

    n1 = np.arange(DFT_N1)
    ang1 = 2.0 * np.pi * (np.outer(n1, n1) % DFT_N1) / DFT_N1
    c1, s1 = np.cos(ang1), np.sin(ang1)
    m1 = np.stack([np.concatenate([c1, s1], axis=1),
                   np.concatenate([-s1, c1], axis=1)], axis=1).reshape(2 * DFT_N1, 2 * DFT_N1)
    m1 = m1 * 2.0 ** -3

    k = n1[:, None] + DFT_N1 * np.arange(DFT_N2)[None, :]
    n2 = np.arange(DFT_N2)
    ang2 = 2.0 * np.pi * ((k[:, :, None] * n2[None, None, :]) % SEQ) / SEQ
    t12 = np.concatenate([np.cos(ang2), np.sin(ang2)], axis=2) * 2.0 ** -3.5
    as_bf16 = lambda a: jnp.asarray(a, dtype=F32).astype(BF16)
    return as_bf16(cs), as_bf16(m1), as_bf16(t12)


def kernel(x, c, norm_g, ada_w, ada_b, attn_w_in, attn_q_gain, attn_k_gain, attn_w_out,
           fourier_w_in, fourier_w_out, final_g):
    b, s, d = x.shape
    t = b * s

    c_pad = jnp.zeros((8, d), F32).at[:b].set(c)
    mod = _adaln(c_pad, ada_w, ada_b)[:, :b]
    shift = mod[:, :, None, :d]
    scale = mod[:, :, None, d:2 * d]
    gate = mod[:, :, None, 2 * d:]

    q_gain = attn_q_gain[0] * (math.log2(math.e) / math.sqrt(HEAD_DIM))
    k_gain = attn_k_gain[0]
    gain_vec = jnp.concatenate([jnp.tile(q_gain, N_HEADS), jnp.tile(k_gain, N_KV_HEADS)])[None, :]
    cos_il, sin_a, sin_b = _rope_tables()

    x2d = x.reshape(t, d)
    qt, k, vt, sg0 = _attn_inproj(x2d, norm_g[0][None, :], scale[0], shift[0],
                                  attn_w_in[0].astype(BF16), gain_vec, cos_il, sin_a, sin_b)
    score_bound = (HEAD_DIM * BF16_NORM_SLACK * jnp.max(jnp.abs(q_gain)) * jnp.max(jnp.abs(k_gain)))
    bounded = (score_bound <= SCORE_BOUND_LOG2).astype(jnp.int32).reshape(1)
    og = _attention(bounded, qt, k, vt, sg0)
    x1 = _outproj_residual(og.reshape(t, ATTN_WIDTH), attn_w_out[0].astype(BF16), x2d, gate[0])

    cs, m1, t12 = _dft_tables()
    z, sg = _fourier_inproj(x1, norm_g[1][None, :], scale[1], shift[1],
                            fourier_w_in[0].astype(BF16), cs)
    fgate = _seq_dft_gate(z, sg, m1, t12)
    out = _fourier_outproj_final(fgate.reshape(t, d), fourier_w_out[0].astype(BF16), x1,
                                 gate[1], final_g[None, :])
    return out.reshape(b, s, d)
```

```python
import math

import numpy as np
import jax
import jax.numpy as jnp
from jax import lax
from jax.experimental import pallas as pl
from jax.experimental.pallas import tpu as pltpu

D_MODEL = 2048
BATCH = 4
SEQ = 8192
GRID_W = 64
HEAD_DIM = 128
N_HEADS = 16
N_KV_HEADS = 4
GQA_GROUP = N_HEADS // N_KV_HEADS
ATTN_WIDTH = N_HEADS * HEAD_DIM
KV_WIDTH = N_KV_HEADS * HEAD_DIM
ATTN_IN_WIDTH = 2 * ATTN_WIDTH + 2 * KV_WIDTH
ROPE_AXIS_DIM = HEAD_DIM // 2
ROPE_THETA = 10000.0
FOURIER_GROUPS = 8
FOURIER_GROUP_W = D_MODEL // FOURIER_GROUPS
EPS = 1e-6

DFT_N1 = 128
DFT_N2 = SEQ // DFT_N1

F32 = jnp.float32
BF16 = jnp.bfloat16

SCORE_BOUND_LOG2 = 64.0
BF16_NORM_SLACK = 1.01

VMEM_LIMIT_BYTES = 56 * 1024 * 1024

ROW_TILE = 512
ATTN_ROW_SUBTILE = 256
FOURIER_ROW_SUBTILE = 512
COL_TILE = 512
ADALN_COL_TILE = 1024
Q_TILE = ROW_TILE
KV_TILE = 256
BOUNDED_CHUNKS_PER_TRIP = 32
ONLINE_CHUNKS_PER_TRIP = 4
DFT_CH_TILE = 128
DFT_N2_PITCH = DFT_N2 + 8
DFT_N1_PITCH = DFT_N1 + 8
DFT_WIDEN_UNROLL = 8
DFT_STAGE1_UNROLL = 16
DFT_STAGE2_UNROLL = 32
DFT_GATE_UNROLL = 4


def _params(*semantics):
    return pltpu.CompilerParams(dimension_semantics=semantics,
                                vmem_limit_bytes=VMEM_LIMIT_BYTES)


def _silu(x):
    return x * jax.nn.sigmoid(x)


def _adaln_body(c_ref, w_ref, b_ref, o_ref):
    c_act = _silu(c_ref[...])
    o_ref[0] = jnp.dot(c_act, w_ref[0], preferred_element_type=F32,
                       precision=lax.Precision.HIGHEST) + b_ref[0]


def _adaln(c_pad, ada_w, ada_b):
    depth, d, n = ada_w.shape
    rows = c_pad.shape[0]
    return pl.pallas_call(
        _adaln_body,
        grid=(depth, n // ADALN_COL_TILE),
        in_specs=[
            pl.BlockSpec((rows, d), lambda l, j: (0, 0)),
            pl.BlockSpec((1, d, ADALN_COL_TILE), lambda l, j: (l, 0, j)),
            pl.BlockSpec((1, 1, ADALN_COL_TILE), lambda l, j: (l, 0, j)),
        ],
        out_specs=pl.BlockSpec((1, rows, ADALN_COL_TILE), lambda l, j: (l, 0, j)),
        out_shape=jax.ShapeDtypeStruct((depth, rows, n), F32),
        compiler_params=_params("arbitrary", "arbitrary"),
        name="adaln",
    )(c_pad, ada_w, ada_b.reshape(depth, 1, n))


def _modulated_norm(x_ref, g_ref, sc_ref, sh_ref, rows):
    x = x_ref[rows, :]
    ms = jnp.mean(x * x, axis=-1, keepdims=True)
    a = g_ref[...] * (1.0 + sc_ref[0])
    return (x * lax.rsqrt(ms + EPS) * a + sh_ref[0]).astype(BF16)


def _row_subtiles(rows_per_matmul):
    return [slice(r, r + rows_per_matmul) for r in range(0, ROW_TILE, rows_per_matmul)]


def _attn_inproj_body(x_ref, g_ref, sc_ref, sh_ref, w_ref, gain_ref, cos_ref, sina_ref, sinb_ref,
                      qt_ref, k_ref, vt_ref, sg_ref):
    heads_per_tile = COL_TILE // HEAD_DIM
    n_q_tiles = ATTN_WIDTH // COL_TILE
    n_qk_tiles = (ATTN_WIDTH + KV_WIDTH) // COL_TILE
    for rows in _row_subtiles(ATTN_ROW_SUBTILE):
        h = _modulated_norm(x_ref, g_ref, sc_ref, sh_ref, rows)
        cos = cos_ref[rows, :]
        sin_a = sina_ref[rows, :]
        sin_b = sinb_ref[rows, :]
        for j in range(ATTN_IN_WIDTH // COL_TILE):
            acc = jnp.dot(h, w_ref[:, j * COL_TILE:(j + 1) * COL_TILE],
                          preferred_element_type=F32)
            for hh in range(heads_per_tile):
                xh = acc[:, hh * HEAD_DIM:(hh + 1) * HEAD_DIM]
                if j < n_qk_tiles:
                    col0 = j * COL_TILE + hh * HEAD_DIM
                    ms = jnp.mean(xh * xh, axis=-1, keepdims=True)
                    xn = xh * lax.rsqrt(ms + EPS) * gain_ref[:, col0:col0 + HEAD_DIM]
                    rot = (xn * cos + pltpu.roll(xn, HEAD_DIM - 1, axis=1) * sin_a
                           + pltpu.roll(xn, 1, axis=1) * sin_b)
                    if j < n_q_tiles:
                        lanes = slice(hh * ROW_TILE + rows.start, hh * ROW_TILE + rows.stop)
                        qt_ref[0, j, 0, :, lanes] = rot.T.astype(BF16)
                    else:
                        k_ref[0, hh, rows, :] = rot.astype(BF16)
                elif j == n_qk_tiles:
                    vt_ref[0, hh, :, rows] = xh.T.astype(BF16)
            if j > n_qk_tiles:
                g0 = (j - n_qk_tiles - 1) * COL_TILE
                sg_ref[0, rows, g0:g0 + COL_TILE] = _silu(acc).astype(BF16)


def _resident(shape):
    return pl.BlockSpec(shape, lambda *_: (0,) * len(shape), pipeline_mode=pl.Buffered(1))


def _attn_inproj(x2d, norm_g, scale, shift, w_bf16, gain_vec, cos_il, sin_a, sin_b):
    t, d = x2d.shape
    n = w_bf16.shape[1]
    tpb = SEQ // ROW_TILE
    rope_spec = pl.BlockSpec((ROW_TILE, HEAD_DIM), lambda i: (i % tpb, 0))
    return pl.pallas_call(
        _attn_inproj_body,
        grid=(t // ROW_TILE,),
        in_specs=[
            pl.BlockSpec((ROW_TILE, d), lambda i: (i, 0)),
            _resident((1, d)),
            pl.BlockSpec((1, 1, d), lambda i: (i // tpb, 0, 0)),
            pl.BlockSpec((1, 1, d), lambda i: (i // tpb, 0, 0)),
            _resident((d, n)),
            _resident(gain_vec.shape),
            rope_spec, rope_spec, rope_spec,
        ],
        out_specs=[
            pl.BlockSpec((1, N_KV_HEADS, 1, HEAD_DIM, GQA_GROUP * ROW_TILE),
                         lambda i: (i // tpb, 0, i % tpb, 0, 0)),
            pl.BlockSpec((1, N_KV_HEADS, ROW_TILE, HEAD_DIM), lambda i: (i // tpb, 0, i % tpb, 0)),
            pl.BlockSpec((1, N_KV_HEADS, HEAD_DIM, ROW_TILE), lambda i: (i // tpb, 0, 0, i % tpb)),
            pl.BlockSpec((1, ROW_TILE, ATTN_WIDTH), lambda i: (i // tpb, i % tpb, 0)),
        ],
        out_shape=[
            jax.ShapeDtypeStruct((BATCH, N_KV_HEADS, tpb, HEAD_DIM, GQA_GROUP * ROW_TILE), BF16),
            jax.ShapeDtypeStruct((BATCH, N_KV_HEADS, SEQ, HEAD_DIM), BF16),
            jax.ShapeDtypeStruct((BATCH, N_KV_HEADS, HEAD_DIM, SEQ), BF16),
            jax.ShapeDtypeStruct((BATCH, SEQ, ATTN_WIDTH), BF16),
        ],
        compiler_params=_params("arbitrary"),
        name="attn_inproj",
    )(x2d, norm_g, scale, shift, w_bf16, gain_vec, cos_il, sin_a, sin_b)


def _attn_body(bounded_ref, qt_ref, k_ref, vt_ref, sg_ref, o_ref, m_scr, l_scr, acc_scr, st_scr):
    n_kv_steps = SEQ // KV_TILE
    qt = qt_ref.at[0, 0, 0]

    def chunk(c):
        start = c * KV_TILE if isinstance(c, int) else pl.multiple_of(c * KV_TILE, KV_TILE)
        return pl.ds(start, KV_TILE)

    def scores(c):
        return jnp.dot(k_ref[0, 0, chunk(c), :], qt[...], preferred_element_type=F32)

    def weighted_values(p, c):
        return jnp.dot(vt_ref[0, 0, :, chunk(c)], p.astype(BF16), preferred_element_type=F32)

    def consume_online(st, c, first):
        del first
        m_old = m_scr[...]
        m_new = jnp.maximum(m_old, jnp.max(st, axis=0, keepdims=True))
        alpha = jnp.exp2(m_old - m_new)
        p = jnp.exp2(st - m_new)
        l_scr[0:1] = alpha * l_scr[0:1] + jnp.sum(p, axis=0, keepdims=True)
        acc_scr[...] = alpha * acc_scr[...] + weighted_values(p, c)
        m_scr[...] = m_new

    def consume_bounded(st, c, first):
        p = jnp.exp2(st)
        part = jnp.sum(p.reshape(KV_TILE // 8, 8, p.shape[1]), axis=0)
        pv = weighted_values(p, c)
        if first:
            l_scr[...] = part
            acc_scr[...] = pv
        else:
            l_scr[...] += part
            acc_scr[...] += pv

    def run(consume, chunks_per_trip):
        if chunks_per_trip == n_kv_steps:
            st = scores(0)
            for j in range(n_kv_steps):
                st_next = scores(j + 1) if j + 1 < n_kv_steps else None
                consume(st, j, j == 0)
                st = st_next
            return

        m_scr[...] = jnp.full(m_scr.shape, -jnp.inf, F32)
        l_scr[...] = jnp.zeros(l_scr.shape, F32)
        acc_scr[...] = jnp.zeros(acc_scr.shape, F32)
        st_scr[...] = scores(0)

        def step(i, carry):
            first = i * chunks_per_trip
            st = st_scr[...]
            for j in range(chunks_per_trip):
                nxt = jnp.minimum(first + j + 1, n_kv_steps - 1)
                st_next = scores(nxt)
                consume(st, first + j, False)
                st = st_next
            st_scr[...] = st
            return carry

        lax.fori_loop(0, n_kv_steps // chunks_per_trip, step, 0)

    @pl.when(bounded_ref[0] != 0)
    def _():
        run(consume_bounded, BOUNDED_CHUNKS_PER_TRIP)

    @pl.when(bounded_ref[0] == 0)
    def _():
        run(consume_online, ONLINE_CHUNKS_PER_TRIP)

    ot = (acc_scr[...] * (1.0 / jnp.sum(l_scr[...], axis=0, keepdims=True))).astype(BF16)
    for hh in range(GQA_GROUP):
        sl = slice(hh * HEAD_DIM, (hh + 1) * HEAD_DIM)
        o = ot[:, hh * Q_TILE:(hh + 1) * Q_TILE].T
        o_ref[0, :, sl] = o * sg_ref[0, :, sl]


def _attention(bounded, qt, k, vt, sg):
    b, _, s, _ = k.shape
    group_w = GQA_GROUP * HEAD_DIM
    nq = GQA_GROUP * Q_TILE
    return pl.pallas_call(
        _attn_body,
        grid=(b, N_KV_HEADS, s // Q_TILE),
        in_specs=[
            pl.BlockSpec(memory_space=pltpu.SMEM),
            pl.BlockSpec((1, 1, 1, HEAD_DIM, nq), lambda bi, kh, qi: (bi, kh, qi, 0, 0)),
            pl.BlockSpec((1, 1, s, HEAD_DIM), lambda bi, kh, qi: (bi, kh, 0, 0)),
            pl.BlockSpec((1, 1, HEAD_DIM, s), lambda bi, kh, qi: (bi, kh, 0, 0)),
            pl.BlockSpec((1, Q_TILE, group_w), lambda bi, kh, qi: (bi, qi, kh)),
        ],
        out_specs=pl.BlockSpec((1, Q_TILE, group_w), lambda bi, kh, qi: (bi, qi, kh)),
        out_shape=jax.ShapeDtypeStruct((b, s, ATTN_WIDTH), BF16),
        scratch_shapes=[
            pltpu.VMEM((1, nq), F32),
            pltpu.VMEM((8, nq), F32),
            pltpu.VMEM((HEAD_DIM, nq), F32),
            pltpu.VMEM((KV_TILE, nq), F32),
        ],
        compiler_params=_params("arbitrary", "arbitrary", "arbitrary"),
        name="attention",
    )(bounded, qt, k, vt, sg)


def _outproj_body(a_ref, w_ref, x_ref, gm_ref, o_ref):
    y = jnp.dot(a_ref[...], w_ref[...], preferred_element_type=F32)
    o_ref[...] = x_ref[...] + gm_ref[0] * y


def _outproj_residual(a2d, w_bf16, x2d, gate_mod):
    t, d = x2d.shape
    tiles_per_batch = SEQ // ROW_TILE
    return pl.pallas_call(
        _outproj_body,
        grid=(t // ROW_TILE,),
        in_specs=[
            pl.BlockSpec((ROW_TILE, a2d.shape[1]), lambda i: (i, 0)),
            pl.BlockSpec(w_bf16.shape, lambda i: (0, 0)),
            pl.BlockSpec((ROW_TILE, d), lambda i: (i, 0)),
            pl.BlockSpec((1, 1, d), lambda i: (i // tiles_per_batch, 0, 0)),
        ],
        out_specs=pl.BlockSpec((ROW_TILE, d), lambda i: (i, 0)),
        out_shape=jax.ShapeDtypeStruct((t, d), F32),
        compiler_params=_params("arbitrary"),
        name="attn_outproj",
    )(a2d, w_bf16, x2d, gate_mod)


def _fourier_inproj_body(x_ref, g_ref, sc_ref, sh_ref, w_ref, cs_ref, z_ref, sg_ref):
    n_u_tiles = D_MODEL // COL_TILE
    for rows in _row_subtiles(FOURIER_ROW_SUBTILE):
        h = _modulated_norm(x_ref, g_ref, sc_ref, sh_ref, rows)
        for j in range(2 * n_u_tiles):
            acc = jnp.dot(h, w_ref[:, j * COL_TILE:(j + 1) * COL_TILE],
                          preferred_element_type=F32)
            if j < n_u_tiles:
                for gg in range(COL_TILE // FOURIER_GROUP_W):
                    sl = slice(j * COL_TILE + gg * FOURIER_GROUP_W,
                               j * COL_TILE + (gg + 1) * FOURIER_GROUP_W)
                    u = acc[:, gg * FOURIER_GROUP_W:(gg + 1) * FOURIER_GROUP_W].astype(BF16)
                    z = jnp.dot(u, cs_ref[...], preferred_element_type=F32)
                    z_ref[0, 0, rows, sl] = z[:, :FOURIER_GROUP_W].astype(BF16)
                    z_ref[0, 1, rows, sl] = z[:, FOURIER_GROUP_W:].astype(BF16)
            else:
                cols = slice((j - n_u_tiles) * COL_TILE, (j - n_u_tiles + 1) * COL_TILE)
                sg_ref[0, rows, cols] = _silu(acc).astype(BF16)


def _fourier_inproj(x2d, norm_g, scale, shift, w_bf16, cs):
    t, d = x2d.shape
    tiles_per_batch = SEQ // ROW_TILE
    return pl.pallas_call(
        _fourier_inproj_body,
        grid=(t // ROW_TILE,),
        in_specs=[
            pl.BlockSpec((ROW_TILE, d), lambda i: (i, 0)),
            _resident((1, d)),
            pl.BlockSpec((1, 1, d), lambda i: (i // tiles_per_batch, 0, 0)),
            pl.BlockSpec((1, 1, d), lambda i: (i // tiles_per_batch, 0, 0)),
            _resident(w_bf16.shape),
            _resident(cs.shape),
        ],
        out_specs=[
            pl.BlockSpec((1, 2, ROW_TILE, D_MODEL),
                         lambda i: (i // tiles_per_batch, 0, i % tiles_per_batch, 0)),
            pl.BlockSpec((1, ROW_TILE, D_MODEL),
                         lambda i: (i // tiles_per_batch, i % tiles_per_batch, 0)),
        ],
        out_shape=[jax.ShapeDtypeStruct((BATCH, 2, SEQ, D_MODEL), BF16),
                   jax.ShapeDtypeStruct((BATCH, SEQ, D_MODEL), BF16)],
        compiler_params=_params("arbitrary"),
        name="fourier_inproj",
    )(x2d, norm_g, scale, shift, w_bf16, cs)


def _seq_dft_body(z_ref, sg_ref, m_ref, t_ref, o_ref, zf_scr, a_scr, f_scr):
    c = DFT_CH_TILE
    im_rows = DFT_N1 * DFT_N2_PITCH

    def widen(i, carry):
        for u in range(DFT_WIDEN_UNROLL):
            n1 = i * DFT_WIDEN_UNROLL + u
            src = pl.multiple_of(n1 * DFT_N2, DFT_N2)
            dst = pl.multiple_of(n1 * DFT_N2_PITCH, 8)
            zf_scr[pl.ds(dst, DFT_N2), :] = z_ref[0, 0, pl.ds(src, DFT_N2), :].astype(F32)
            zf_scr[pl.ds(im_rows + dst, DFT_N2), :] = (
                z_ref[0, 1, pl.ds(src, DFT_N2), :].astype(F32))
        return carry

    lax.fori_loop(0, DFT_N1 // DFT_WIDEN_UNROLL, widen, 0)

    def stage1(i, carry):
        for u in range(DFT_STAGE1_UNROLL):
            n2 = (i * DFT_STAGE1_UNROLL + u) * 2
            halves = []
            for dn in range(2):
                xr = zf_scr[pl.ds(n2 + dn, DFT_N1, stride=DFT_N2_PITCH), :]
                xi = zf_scr[pl.ds(im_rows + n2 + dn, DFT_N1, stride=DFT_N2_PITCH), :]
                halves.append(jnp.concatenate([xr, xi], axis=0))
            x = jnp.concatenate(halves, axis=1).astype(BF16)
            a = jnp.dot(m_ref[...], x, preferred_element_type=F32)
            for dn in range(2):
                a_scr[pl.ds(n2 + dn, 2 * DFT_N1, stride=DFT_N2_PITCH), :] = (
                    a[:, dn * c:(dn + 1) * c])
        return carry

    lax.fori_loop(0, DFT_N2 // (2 * DFT_STAGE1_UNROLL), stage1, 0)

    def stage2(i, carry):
        for u in range(DFT_STAGE2_UNROLL):
            k1 = i * DFT_STAGE2_UNROLL + u
            r = pl.multiple_of(k1 * 2 * DFT_N2_PITCH, 8)
            rows = jnp.concatenate([a_scr[pl.ds(r, DFT_N2), :],
                                    a_scr[pl.ds(r + DFT_N2_PITCH, DFT_N2), :]],
                                   axis=0).astype(BF16)
            f = jnp.dot(t_ref[k1], rows, preferred_element_type=F32)
            f_scr[pl.ds(k1, DFT_N2, stride=DFT_N1_PITCH), :] = f
        return carry

    lax.fori_loop(0, DFT_N1 // DFT_STAGE2_UNROLL, stage2, 0)

    def gate(i, carry):
        for u in range(DFT_GATE_UNROLL):
            k2 = i * DFT_GATE_UNROLL + u
            src = pl.multiple_of(k2 * DFT_N1_PITCH, 8)
            dst = pl.multiple_of(k2 * DFT_N1, DFT_N1)
            o_ref[0, pl.ds(dst, DFT_N1), :] = (
                f_scr[pl.ds(src, DFT_N1), :]
                * sg_ref[0, pl.ds(dst, DFT_N1), :].astype(F32)).astype(BF16)
        return carry

    lax.fori_loop(0, DFT_N2 // DFT_GATE_UNROLL, gate, 0)


def _seq_dft_gate(z, sg, m1, t12):
    b, _, s, d = z.shape
    c = DFT_CH_TILE
    return pl.pallas_call(
        _seq_dft_body,
        grid=(b, d // c),
        in_specs=[
            pl.BlockSpec((1, 2, s, c), lambda bi, ci: (bi, 0, 0, ci)),
            pl.BlockSpec((1, s, c), lambda bi, ci: (bi, 0, ci)),
            _resident(m1.shape),
            _resident(t12.shape),
        ],
        out_specs=pl.BlockSpec((1, s, c), lambda bi, ci: (bi, 0, ci)),
        out_shape=jax.ShapeDtypeStruct((b, s, d), BF16),
        scratch_shapes=[
            pltpu.VMEM((2 * DFT_N1 * DFT_N2_PITCH, c), F32),
            pltpu.VMEM((2 * DFT_N1 * DFT_N2_PITCH, c), F32),
            pltpu.VMEM((DFT_N2 * DFT_N1_PITCH, c), F32),
        ],
        compiler_params=_params("arbitrary", "arbitrary"),
        name="seq_dft_gate",
    )(z, sg, m1, t12)


def _final_body(a_ref, w_ref, x_ref, gm_ref, fg_ref, o_ref):
    y = jnp.dot(a_ref[...], w_ref[...], preferred_element_type=F32)
    x2 = x_ref[...] + gm_ref[0] * y
    ms = jnp.mean(x2 * x2, axis=-1, keepdims=True)
    o_ref[...] = x2 * lax.rsqrt(ms + EPS) * fg_ref[...]


def _fourier_outproj_final(a2d, w_bf16, x2d, gate_mod, final_g):
    t, d = x2d.shape
    tiles_per_batch = SEQ // ROW_TILE
    return pl.pallas_call(
        _final_body,
        grid=(t // ROW_TILE,),
        in_specs=[
            pl.BlockSpec((ROW_TILE, a2d.shape[1]), lambda i: (i, 0)),
            _resident(w_bf16.shape),
            pl.BlockSpec((ROW_TILE, d), lambda i: (i, 0)),
            pl.BlockSpec((1, 1, d), lambda i: (i // tiles_per_batch, 0, 0)),
            _resident((1, d)),
        ],
        out_specs=pl.BlockSpec((ROW_TILE, d), lambda i: (i, 0)),
        out_shape=jax.ShapeDtypeStruct((t, d), F32),
        compiler_params=_params("arbitrary"),
        name="fourier_outproj_final",
    )(a2d, w_bf16, x2d, gate_mod, final_g)


def _rope_tables():
    rows = SEQ // GRID_W
    inv_freq = ROPE_THETA ** (-jnp.arange(0, ROPE_AXIS_DIM, 2, dtype=F32) / ROPE_AXIS_DIM)
    n_freq = inv_freq.shape[0]
    pair = np.arange(HEAD_DIM) // 2
    lane_freq = inv_freq[pair % n_freq][None, :]
    uses_row = jnp.asarray(pair < n_freq)[None, None, :]
    even_lane = jnp.asarray(np.arange(HEAD_DIM) % 2 == 0)[None, None, :]
    row_ang = jnp.arange(rows).astype(F32)[:, None] * lane_freq
    col_ang = jnp.arange(GRID_W).astype(F32)[:, None] * lane_freq

    def per_position(fn):
        return jnp.where(uses_row, fn(row_ang)[:, None, :], fn(col_ang)[None, :, :])

    cos, sin = per_position(jnp.cos), per_position(jnp.sin)
    flat = lambda a: a.reshape(SEQ, HEAD_DIM)
    return (flat(cos), flat(jnp.where(even_lane, -sin, 0.0)), flat(jnp.where(even_lane, 0.0, sin)))


def _dft_tables():
    w = np.arange(FOURIER_GROUP_W)
    ang_w = 2.0 * np.pi * (np.outer(w, w) % FOURIER_GROUP_W) / FOURIER_GROUP_W
    cs = np.concatenate([np.cos(ang_w), -np.sin(ang_w)], axis=1) * 2.0 ** -4

    n1 = np.arange(DFT_N1)
    ang1 = 2.0 * np.pi * (np.outer(n1, n1) % DFT_N1) / DFT_N1
    c1, s1 = np.cos(ang1), np.sin(ang1)
    m1 = np.stack([np.concatenate([c1, s1], axis=1),
                   np.concatenate([-s1, c1], axis=1)], axis=1).reshape(2 * DFT_N1, 2 * DFT_N1)
    m1 = m1 * 2.0 ** -3

    k = n1[:, None] + DFT_N1 * np.arange(DFT_N2)[None, :]
    n2 = np.arange(DFT_N2)
    ang2 = 2.0 * np.pi * ((k[:, :, None] * n2[None, None, :]) % SEQ) / SEQ
    t12 = np.concatenate([np.cos(ang2), np.sin(ang2)], axis=2) * 2.0 ** -3.5
    as_bf16 = lambda a: jnp.asarray(a, dtype=F32).astype(BF16)
    return as_bf16(cs), as_bf16(m1), as_bf16(t12)


def kernel(x, c, norm_g, ada_w, ada_b, attn_w_in, attn_q_gain, attn_k_gain, attn_w_out,
           fourier_w_in, fourier_w_out, final_g):
    b, s, d = x.shape
    t = b * s

    c_pad = jnp.zeros((8, d), F32).at[:b].set(c)
    mod = _adaln(c_pad, ada_w, ada_b)[:, :b]
    shift = mod[:, :, None, :d]
    scale = mod[:, :, None, d:2 * d]
    gate = mod[:, :, None, 2 * d:]

    q_gain = attn_q_gain[0] * (math.log2(math.e) / math.sqrt(HEAD_DIM))
    k_gain = attn_k_gain[0]
    gain_vec = jnp.concatenate([jnp.tile(q_gain, N_HEADS), jnp.tile(k_gain, N_KV_HEADS)])[None, :]
    cos_il, sin_a, sin_b = _rope_tables()

    x2d = x.reshape(t, d)
    qt, k, vt, sg0 = _attn_inproj(x2d, norm_g[0][None, :], scale[0], shift[0],
                                  attn_w_in[0].astype(BF16), gain_vec, cos_il, sin_a, sin_b)
    score_bound = (HEAD_DIM * BF16_NORM_SLACK * jnp.max(jnp.abs(q_gain)) * jnp.max(jnp.abs(k_gain)))
    bounded = (score_bound <= SCORE_BOUND_LOG2).astype(jnp.int32).reshape(1)
    og = _attention(bounded, qt, k, vt, sg0)
    x1 = _outproj_residual(og.reshape(t, ATTN_WIDTH), attn_w_out[0].astype(BF16), x2d, gate[0])

    cs, m1, t12 = _dft_tables()
    z, sg = _fourier_inproj(x1, norm_g[1][None, :], scale[1], shift[1],
                            fourier_w_in[0].astype(BF16), cs)
    fgate = _seq_dft_gate(z, sg, m1, t12)
    out = _fourier_outproj_final(fgate.reshape(t, d), fourier_w_out[0].astype(BF16), x1,
                                 gate[1], final_g[None, :])
    return out.reshape(b, s, d)
```

```python
import math

import numpy as np
import jax
import jax.numpy as jnp
from jax import lax
from jax.experimental import pallas as pl
from jax.experimental.pallas import tpu as pltpu

D_MODEL = 2048
BATCH = 4
SEQ = 8192
GRID_W = 64
HEAD_DIM = 128
N_HEADS = 16
N_KV_HEADS = 4
GQA_GROUP = N_HEADS // N_KV_HEADS
ATTN_WIDTH = N_HEADS * HEAD_DIM
KV_WIDTH = N_KV_HEADS * HEAD_DIM
ATTN_IN_WIDTH = 2 * ATTN_WIDTH + 2 * KV_WIDTH
ROPE_AXIS_DIM = HEAD_DIM // 2
ROPE_THETA = 10000.0
FOURIER_GROUPS = 8
FOURIER_GROUP_W = D_MODEL // FOURIER_GROUPS
EPS = 1e-6

DFT_N1 = 128
DFT_N2 = SEQ // DFT_N1

F32 = jnp.float32
BF16 = jnp.bfloat16

SCORE_BOUND_LOG2 = 64.0
BF16_NORM_SLACK = 1.01

VMEM_LIMIT_BYTES = 56 * 1024 * 1024

ROW_TILE = 512
ATTN_ROW_SUBTILE = 256
FOURIER_ROW_SUBTILE = 512
COL_TILE = 512
ADALN_COL_TILE = 1024
Q_TILE = ROW_TILE
KV_TILE = 256
BOUNDED_CHUNKS_PER_TRIP = 32
ONLINE_CHUNKS_PER_TRIP = 4
DFT_CH_TILE = 128
DFT_N2_PITCH = DFT_N2 + 8
DFT_N1_PITCH = DFT_N1 + 8
DFT_WIDEN_UNROLL = 8
DFT_STAGE1_UNROLL = 16
DFT_STAGE2_UNROLL = 32
DFT_GATE_UNROLL = 4


def _params(*semantics):
    return pltpu.CompilerParams(dimension_semantics=semantics,
                                vmem_limit_bytes=VMEM_LIMIT_BYTES)


def _silu(x):
    return x * jax.nn.sigmoid(x)


def _adaln_body(ct_ref, w_ref, b_ref, o_ref):
    w = w_ref[0]
    c_act = _silu(ct_ref[...])
    for bi in range(ct_ref.shape[1]):
        o_ref[0, bi:bi + 1, :] = (jnp.sum(w * c_act[:, bi:bi + 1], axis=0, keepdims=True)
                                  + b_ref[0])


def _adaln(c_t, ada_w, ada_b):
    depth, d, n = ada_w.shape
    b = c_t.shape[1]
    return pl.pallas_call(
        _adaln_body,
        grid=(depth, n // ADALN_COL_TILE),
        in_specs=[
            pl.BlockSpec((d, b), lambda l, j: (0, 0)),
            pl.BlockSpec((1, d, ADALN_COL_TILE), lambda l, j: (l, 0, j)),
            pl.BlockSpec((1, 1, ADALN_COL_TILE), lambda l, j: (l, 0, j)),
        ],
        out_specs=pl.BlockSpec((1, b, ADALN_COL_TILE), lambda l, j: (l, 0, j)),
        out_shape=jax.ShapeDtypeStruct((depth, b, n), F32),
        compiler_params=_params("arbitrary", "arbitrary"),
        name="adaln",
    )(c_t, ada_w, ada_b.reshape(depth, 1, n))


def _modulated_norm(x_ref, g_ref, sc_ref, sh_ref, rows):
    x = x_ref[rows, :]
    ms = jnp.mean(x * x, axis=-1, keepdims=True)
    a = g_ref[...] * (1.0 + sc_ref[0])
    return (x * lax.rsqrt(ms + EPS) * a + sh_ref[0]).astype(BF16)


def _row_subtiles(rows_per_matmul):
    return [slice(r, r + rows_per_matmul) for r in range(0, ROW_TILE, rows_per_matmul)]


def _attn_inproj_body(x_ref, g_ref, sc_ref, sh_ref, w_ref, gain_ref, cos_ref, sina_ref, sinb_ref,
                      qt_ref, k_ref, vt_ref, sg_ref):
    heads_per_tile = COL_TILE // HEAD_DIM
    n_q_tiles = ATTN_WIDTH // COL_TILE
    n_qk_tiles = (ATTN_WIDTH + KV_WIDTH) // COL_TILE
    for rows in _row_subtiles(ATTN_ROW_SUBTILE):
        h = _modulated_norm(x_ref, g_ref, sc_ref, sh_ref, rows)
        cos = cos_ref[rows, :]
        sin_a = sina_ref[rows, :]
        sin_b = sinb_ref[rows, :]
        for j in range(ATTN_IN_WIDTH // COL_TILE):
            acc = jnp.dot(h, w_ref[:, j * COL_TILE:(j + 1) * COL_TILE],
                          preferred_element_type=F32)
            for hh in range(heads_per_tile):
                xh = acc[:, hh * HEAD_DIM:(hh + 1) * HEAD_DIM]
                if j < n_qk_tiles:
                    col0 = j * COL_TILE + hh * HEAD_DIM
                    ms = jnp.mean(xh * xh, axis=-1, keepdims=True)
                    xn = xh * lax.rsqrt(ms + EPS) * gain_ref[:, col0:col0 + HEAD_DIM]
                    rot = (xn * cos + pltpu.roll(xn, HEAD_DIM - 1, axis=1) * sin_a
                           + pltpu.roll(xn, 1, axis=1) * sin_b)
                    if j < n_q_tiles:
                        lanes = slice(hh * ROW_TILE + rows.start, hh * ROW_TILE + rows.stop)
                        qt_ref[0, j, 0, :, lanes] = rot.T.astype(BF16)
                    else:
                        k_ref[0, hh, rows, :] = rot.astype(BF16)
                elif j == n_qk_tiles:
                    vt_ref[0, hh, :, rows] = xh.T.astype(BF16)
            if j > n_qk_tiles:
                g0 = (j - n_qk_tiles - 1) * COL_TILE
                sg_ref[0, rows, g0:g0 + COL_TILE] = _silu(acc).astype(BF16)


def _resident(shape):
    return pl.BlockSpec(shape, lambda *_: (0,) * len(shape), pipeline_mode=pl.Buffered(1))


def _attn_inproj(x2d, norm_g, scale, shift, w_bf16, gain_vec, cos_il, sin_a, sin_b):
    t, d = x2d.shape
    n = w_bf16.shape[1]
    tpb = SEQ // ROW_TILE
    rope_spec = pl.BlockSpec((ROW_TILE, HEAD_DIM), lambda i: (i % tpb, 0))
    return pl.pallas_call(
        _attn_inproj_body,
        grid=(t // ROW_TILE,),
        in_specs=[
            pl.BlockSpec((ROW_TILE, d), lambda i: (i, 0)),
            _resident((1, d)),
            pl.BlockSpec((1, 1, d), lambda i: (i // tpb, 0, 0)),
            pl.BlockSpec((1, 1, d), lambda i: (i // tpb, 0, 0)),
            _resident((d, n)),
            _resident(gain_vec.shape),
            rope_spec, rope_spec, rope_spec,
        ],
        out_specs=[
            pl.BlockSpec((1, N_KV_HEADS, 1, HEAD_DIM, GQA_GROUP * ROW_TILE),
                         lambda i: (i // tpb, 0, i % tpb, 0, 0)),
            pl.BlockSpec((1, N_KV_HEADS, ROW_TILE, HEAD_DIM), lambda i: (i // tpb, 0, i % tpb, 0)),
            pl.BlockSpec((1, N_KV_HEADS, HEAD_DIM, ROW_TILE), lambda i: (i // tpb, 0, 0, i % tpb)),
            pl.BlockSpec((1, ROW_TILE, ATTN_WIDTH), lambda i: (i // tpb, i % tpb, 0)),
        ],
        out_shape=[
            jax.ShapeDtypeStruct((BATCH, N_KV_HEADS, tpb, HEAD_DIM, GQA_GROUP * ROW_TILE), BF16),
            jax.ShapeDtypeStruct((BATCH, N_KV_HEADS, SEQ, HEAD_DIM), BF16),
            jax.ShapeDtypeStruct((BATCH, N_KV_HEADS, HEAD_DIM, SEQ), BF16),
            jax.ShapeDtypeStruct((BATCH, SEQ, ATTN_WIDTH), BF16),
        ],
        compiler_params=_params("arbitrary"),
        name="attn_inproj",
    )(x2d, norm_g, scale, shift, w_bf16, gain_vec, cos_il, sin_a, sin_b)


def _attn_body(bounded_ref, qt_ref, k_ref, vt_ref, sg_ref, o_ref, m_scr, l_scr, acc_scr, st_scr):
    n_kv_steps = SEQ // KV_TILE
    qt = qt_ref.at[0, 0, 0]

    def chunk(c):
        start = c * KV_TILE if isinstance(c, int) else pl.multiple_of(c * KV_TILE, KV_TILE)
        return pl.ds(start, KV_TILE)

    def scores(c):
        return jnp.dot(k_ref[0, 0, chunk(c), :], qt[...], preferred_element_type=F32)

    def weighted_values(p, c):
        return jnp.dot(vt_ref[0, 0, :, chunk(c)], p.astype(BF16), preferred_element_type=F32)

    def consume_online(st, c, first):
        del first
        m_old = m_scr[...]
        m_new = jnp.maximum(m_old, jnp.max(st, axis=0, keepdims=True))
        alpha = jnp.exp2(m_old - m_new)
        p = jnp.exp2(st - m_new)
        l_scr[0:1] = alpha * l_scr[0:1] + jnp.sum(p, axis=0, keepdims=True)
        acc_scr[...] = alpha * acc_scr[...] + weighted_values(p, c)
        m_scr[...] = m_new

    def consume_bounded(st, c, first):
        p = jnp.exp2(st)
        part = jnp.sum(p.reshape(KV_TILE // 8, 8, p.shape[1]), axis=0)
        pv = weighted_values(p, c)
        if first:
            l_scr[...] = part
            acc_scr[...] = pv
        else:
            l_scr[...] += part
            acc_scr[...] += pv

    def run(consume, chunks_per_trip):
        if chunks_per_trip == n_kv_steps:
            st = scores(0)
            for j in range(n_kv_steps):
                st_next = scores(j + 1) if j + 1 < n_kv_steps else None
                consume(st, j, j == 0)
                st = st_next
            return

        m_scr[...] = jnp.full(m_scr.shape, -jnp.inf, F32)
        l_scr[...] = jnp.zeros(l_scr.shape, F32)
        acc_scr[...] = jnp.zeros(acc_scr.shape, F32)
        st_scr[...] = scores(0)

        def step(i, carry):
            first = i * chunks_per_trip
            st = st_scr[...]
            for j in range(chunks_per_trip):
                nxt = jnp.minimum(first + j + 1, n_kv_steps - 1)
                st_next = scores(nxt)
                consume(st, first + j, False)
                st = st_next
            st_scr[...] = st
            return carry

        lax.fori_loop(0, n_kv_steps // chunks_per_trip, step, 0)

    @pl.when(bounded_ref[0] != 0)
    def _():
        run(consume_bounded, BOUNDED_CHUNKS_PER_TRIP)

    @pl.when(bounded_ref[0] == 0)
    def _():
        run(consume_online, ONLINE_CHUNKS_PER_TRIP)

    ot = (acc_scr[...] * (1.0 / jnp.sum(l_scr[...], axis=0, keepdims=True))).astype(BF16)
    for hh in range(GQA_GROUP):
        sl = slice(hh * HEAD_DIM, (hh + 1) * HEAD_DIM)
        o = ot[:, hh * Q_TILE:(hh + 1) * Q_TILE].T
        o_ref[0, :, sl] = o * sg_ref[0, :, sl]


def _attention(bounded, qt, k, vt, sg):
    b, _, s, _ = k.shape
    group_w = GQA_GROUP * HEAD_DIM
    nq = GQA_GROUP * Q_TILE
    return pl.pallas_call(
        _attn_body,
        grid=(b, N_KV_HEADS, s // Q_TILE),
        in_specs=[
            pl.BlockSpec(memory_space=pltpu.SMEM),
            pl.BlockSpec((1, 1, 1, HEAD_DIM, nq), lambda bi, kh, qi: (bi, kh, qi, 0, 0)),
            pl.BlockSpec((1, 1, s, HEAD_DIM), lambda bi, kh, qi: (bi, kh, 0, 0)),
            pl.BlockSpec((1, 1, HEAD_DIM, s), lambda bi, kh, qi: (bi, kh, 0, 0)),
            pl.BlockSpec((1, Q_TILE, group_w), lambda bi, kh, qi: (bi, qi, kh)),
        ],
        out_specs=pl.BlockSpec((1, Q_TILE, group_w), lambda bi, kh, qi: (bi, qi, kh)),
        out_shape=jax.ShapeDtypeStruct((b, s, ATTN_WIDTH), BF16),
        scratch_shapes=[
            pltpu.VMEM((1, nq), F32),
            pltpu.VMEM((8, nq), F32),
            pltpu.VMEM((HEAD_DIM, nq), F32),
            pltpu.VMEM((KV_TILE, nq), F32),
        ],
        compiler_params=_params("arbitrary", "arbitrary", "arbitrary"),
        name="attention",
    )(bounded, qt, k, vt, sg)


def _outproj_body(a_ref, w_ref, x_ref, gm_ref, o_ref):
    y = jnp.dot(a_ref[...], w_ref[...], preferred_element_type=F32)
    o_ref[...] = x_ref[...] + gm_ref[0] * y


def _outproj_residual(a2d, w_bf16, x2d, gate_mod):
    t, d = x2d.shape
    tiles_per_batch = SEQ // ROW_TILE
    return pl.pallas_call(
        _outproj_body,
        grid=(t // ROW_TILE,),
        in_specs=[
            pl.BlockSpec((ROW_TILE, a2d.shape[1]), lambda i: (i, 0)),
            pl.BlockSpec(w_bf16.shape, lambda i: (0, 0)),
            pl.BlockSpec((ROW_TILE, d), lambda i: (i, 0)),
            pl.BlockSpec((1, 1, d), lambda i: (i // tiles_per_batch, 0, 0)),
        ],
        out_specs=pl.BlockSpec((ROW_TILE, d), lambda i: (i, 0)),
        out_shape=jax.ShapeDtypeStruct((t, d), F32),
        compiler_params=_params("arbitrary"),
        name="attn_outproj",
    )(a2d, w_bf16, x2d, gate_mod)


def _fourier_inproj_body(x_ref, g_ref, sc_ref, sh_ref, w_ref, cs_ref, z_ref, sg_ref):
    n_u_tiles = D_MODEL // COL_TILE
    for rows in _row_subtiles(FOURIER_ROW_SUBTILE):
        h = _modulated_norm(x_ref, g_ref, sc_ref, sh_ref, rows)
        for j in range(2 * n_u_tiles):
            acc = jnp.dot(h, w_ref[:, j * COL_TILE:(j + 1) * COL_TILE],
                          preferred_element_type=F32)
            if j < n_u_tiles:
                for gg in range(COL_TILE // FOURIER_GROUP_W):
                    sl = slice(j * COL_TILE + gg * FOURIER_GROUP_W,
                               j * COL_TILE + (gg + 1) * FOURIER_GROUP_W)
                    u = acc[:, gg * FOURIER_GROUP_W:(gg + 1) * FOURIER_GROUP_W].astype(BF16)
                    z = jnp.dot(u, cs_ref[...], preferred_element_type=F32)
                    z_ref[0, 0, rows, sl] = z[:, :FOURIER_GROUP_W].astype(BF16)
                    z_ref[0, 1, rows, sl] = z[:, FOURIER_GROUP_W:].astype(BF16)
            else:
                cols = slice((j - n_u_tiles) * COL_TILE, (j - n_u_tiles + 1) * COL_TILE)
                sg_ref[0, rows, cols] = _silu(acc).astype(BF16)


def _fourier_inproj(x2d, norm_g, scale, shift, w_bf16, cs):
    t, d = x2d.shape
    tiles_per_batch = SEQ // ROW_TILE
    return pl.pallas_call(
        _fourier_inproj_body,
        grid=(t // ROW_TILE,),
        in_specs=[
            pl.BlockSpec((ROW_TILE, d), lambda i: (i, 0)),
            _resident((1, d)),
            pl.BlockSpec((1, 1, d), lambda i: (i // tiles_per_batch, 0, 0)),
            pl.BlockSpec((1, 1, d), lambda i: (i // tiles_per_batch, 0, 0)),
            _resident(w_bf16.shape),
            _resident(cs.shape),
        ],
        out_specs=[
            pl.BlockSpec((1, 2, ROW_TILE, D_MODEL),
                         lambda i: (i // tiles_per_batch, 0, i % tiles_per_batch, 0)),
            pl.BlockSpec((1, ROW_TILE, D_MODEL),
                         lambda i: (i // tiles_per_batch, i % tiles_per_batch, 0)),
        ],
        out_shape=[jax.ShapeDtypeStruct((BATCH, 2, SEQ, D_MODEL), BF16),
                   jax.ShapeDtypeStruct((BATCH, SEQ, D_MODEL), BF16)],
        compiler_params=_params("arbitrary"),
        name="fourier_inproj",
    )(x2d, norm_g, scale, shift, w_bf16, cs)


def _seq_dft_body(z_ref, sg_ref, m_ref, t_ref, o_ref, zf_scr, a_scr, f_scr):
    c = DFT_CH_TILE
    im_rows = DFT_N1 * DFT_N2_PITCH

    def widen(i, carry):
        for u in range(DFT_WIDEN_UNROLL):
            n1 = i * DFT_WIDEN_UNROLL + u
            src = pl.multiple_of(n1 * DFT_N2, DFT_N2)
            dst = pl.multiple_of(n1 * DFT_N2_PITCH, 8)
            zf_scr[pl.ds(dst, DFT_N2), :] = z_ref[0, 0, pl.ds(src, DFT_N2), :].astype(F32)
            zf_scr[pl.ds(im_rows + dst, DFT_N2), :] = (
                z_ref[0, 1, pl.ds(src, DFT_N2), :].astype(F32))
        return carry

    lax.fori_loop(0, DFT_N1 // DFT_WIDEN_UNROLL, widen, 0)

    def stage1(i, carry):
        for u in range(DFT_STAGE1_UNROLL):
            n2 = (i * DFT_STAGE1_UNROLL + u) * 2
            halves = []
            for dn in range(2):
                xr = zf_scr[pl.ds(n2 + dn, DFT_N1, stride=DFT_N2_PITCH), :]
                xi = zf_scr[pl.ds(im_rows + n2 + dn, DFT_N1, stride=DFT_N2_PITCH), :]
                halves.append(jnp.concatenate([xr, xi], axis=0))
            x = jnp.concatenate(halves, axis=1).astype(BF16)
            a = jnp.dot(m_ref[...], x, preferred_element_type=F32)
            for dn in range(2):
                a_scr[pl.ds(n2 + dn, 2 * DFT_N1, stride=DFT_N2_PITCH), :] = (
                    a[:, dn * c:(dn + 1) * c])
        return carry

    lax.fori_loop(0, DFT_N2 // (2 * DFT_STAGE1_UNROLL), stage1, 0)

    def stage2(i, carry):
        for u in range(DFT_STAGE2_UNROLL):
            k1 = i * DFT_STAGE2_UNROLL + u
            r = pl.multiple_of(k1 * 2 * DFT_N2_PITCH, 8)
            rows = jnp.concatenate([a_scr[pl.ds(r, DFT_N2), :],
                                    a_scr[pl.ds(r + DFT_N2_PITCH, DFT_N2), :]],
                                   axis=0).astype(BF16)
            f = jnp.dot(t_ref[k1], rows, preferred_element_type=F32)
            f_scr[pl.ds(k1, DFT_N2, stride=DFT_N1_PITCH), :] = f
        return carry

    lax.fori_loop(0, DFT_N1 // DFT_STAGE2_UNROLL, stage2, 0)

    def gate(i, carry):
        for u in range(DFT_GATE_UNROLL):
            k2 = i * DFT_GATE_UNROLL + u
            src = pl.multiple_of(k2 * DFT_N1_PITCH, 8)
            dst = pl.multiple_of(k2 * DFT_N1, DFT_N1)
            o_ref[0, pl.ds(dst, DFT_N1), :] = (
                f_scr[pl.ds(src, DFT_N1), :]
                * sg_ref[0, pl.ds(dst, DFT_N1), :].astype(F32)).astype(BF16)
        return carry

    lax.fori_loop(0, DFT_N2 // DFT_GATE_UNROLL, gate, 0)


def _seq_dft_gate(z, sg, m1, t12):
    b, _, s, d = z.shape
    c = DFT_CH_TILE
    return pl.pallas_call(
        _seq_dft_body,
        grid=(b, d // c),
        in_specs=[
            pl.BlockSpec((1, 2, s, c), lambda bi, ci: (bi, 0, 0, ci)),
            pl.BlockSpec((1, s, c), lambda bi, ci: (bi, 0, ci)),
            _resident(m1.shape),
            _resident(t12.shape),
        ],
        out_specs=pl.BlockSpec((1, s, c), lambda bi, ci: (bi, 0, ci)),
        out_shape=jax.ShapeDtypeStruct((b, s, d), BF16),
        scratch_shapes=[
            pltpu.VMEM((2 * DFT_N1 * DFT_N2_PITCH, c), F32),
            pltpu.VMEM((2 * DFT_N1 * DFT_N2_PITCH, c), F32),
            pltpu.VMEM((DFT_N2 * DFT_N1_PITCH, c), F32),
        ],
        compiler_params=_params("arbitrary", "arbitrary"),
        name="seq_dft_gate",
    )(z, sg, m1, t12)


def _final_body(a_ref, w_ref, x_ref, gm_ref, fg_ref, o_ref):
    y = jnp.dot(a_ref[...], w_ref[...], preferred_element_type=F32)
    x2 = x_ref[...] + gm_ref[0] * y
    ms = jnp.mean(x2 * x2, axis=-1, keepdims=True)
    o_ref[...] = x2 * lax.rsqrt(ms + EPS) * fg_ref[...]


def _fourier_outproj_final(a2d, w_bf16, x2d, gate_mod, final_g):
    t, d = x2d.shape
    tiles_per_batch = SEQ // ROW_TILE
    return pl.pallas_call(
        _final_body,
        grid=(t // ROW_TILE,),
        in_specs=[
            pl.BlockSpec((ROW_TILE, a2d.shape[1]), lambda i: (i, 0)),
            _resident(w_bf16.shape),
            pl.BlockSpec((ROW_TILE, d), lambda i: (i, 0)),
            pl.BlockSpec((1, 1, d), lambda i: (i // tiles_per_batch, 0, 0)),
            _resident((1, d)),
        ],
        out_specs=pl.BlockSpec((ROW_TILE, d), lambda i: (i, 0)),
        out_shape=jax.ShapeDtypeStruct((t, d), F32),
        compiler_params=_params("arbitrary"),
        name="fourier_outproj_final",
    )(a2d, w_bf16, x2d, gate_mod, final_g)


def _rope_tables():
    rows = SEQ // GRID_W
    inv_freq = ROPE_THETA ** (-jnp.arange(0, ROPE_AXIS_DIM, 2, dtype=F32) / ROPE_AXIS_DIM)
    n_freq = inv_freq.shape[0]
    pair = np.arange(HEAD_DIM) // 2
    lane_freq = inv_freq[pair % n_freq][None, :]
    uses_row = jnp.asarray(pair < n_freq)[None, None, :]
    even_lane = jnp.asarray(np.arange(HEAD_DIM) % 2 == 0)[None, None, :]
    row_ang = jnp.arange(rows).astype(F32)[:, None] * lane_freq
    col_ang = jnp.arange(GRID_W).astype(F32)[:, None] * lane_freq

    def per_position(fn):
        return jnp.where(uses_row, fn(row_ang)[:, None, :], fn(col_ang)[None, :, :])

    cos, sin = per_position(jnp.cos), per_position(jnp.sin)
    flat = lambda a: a.reshape(SEQ, HEAD_DIM)
    return (flat(cos), flat(jnp.where(even_lane, -sin, 0.0)), flat(jnp.where(even_lane, 0.0, sin)))


def _dft_tables():
    w = np.arange(FOURIER_GROUP_W)
    ang_w = 2.0 * np.pi * (np.outer(w, w) % FOURIER_GROUP_W) / FOURIER_GROUP_W
    cs = np.concatenate([np.cos(ang_w), -np.sin(ang_w)], axis=1) * 2.0 ** -4

    n1 = np.arange(DFT_N1)
    ang1 = 2.0 * np.pi * (np.outer(n1, n1) % DFT_N1) / DFT_N1
    c1, s1 = np.cos(ang1), np.sin(ang1)
    m1 = np.stack([np.concatenate([c1, s1], axis=1),
                   np.concatenate([-s1, c1], axis=1)], axis=1).reshape(2 * DFT_N1, 2 * DFT_N1)
    m1 = m1 * 2.0 ** -3

    k = n1[:, None] + DFT_N1 * np.arange(DFT_N2)[None, :]
    n2 = np.arange(DFT_N2)
    ang2 = 2.0 * np.pi * ((k[:, :, None] * n2[None, None, :]) % SEQ) / SEQ
    t12 = np.concatenate([np.cos(ang2), np.sin(ang2)], axis=2) * 2.0 ** -3.5
    as_bf16 = lambda a: jnp.asarray(a, dtype=F32).astype(BF16)
    return as_bf16(cs), as_bf16(m1), as_bf16(t12)


def kernel(x, c, norm_g, ada_w, ada_b, attn_w_in, attn_q_gain, attn_k_gain, attn_w_out,
           fourier_w_in, fourier_w_out, final_g):
    b, s, d = x.shape
    t = b * s

    mod = _adaln(c.T, ada_w, ada_b)
    shift = mod[:, :, None, :d]
    scale = mod[:, :, None, d:2 * d]
    gate = mod[:, :, None, 2 * d:]

    q_gain = attn_q_gain[0] * (math.log2(math.e) / math.sqrt(HEAD_DIM))
    k_gain = attn_k_gain[0]
    gain_vec = jnp.concatenate([jnp.tile(q_gain, N_HEADS), jnp.tile(k_gain, N_KV_HEADS)])[None, :]
    cos_il, sin_a, sin_b = _rope_tables()

    x2d = x.reshape(t, d)
    qt, k, vt, sg0 = _attn_inproj(x2d, norm_g[0][None, :], scale[0], shift[0],
                                  attn_w_in[0].astype(BF16), gain_vec, cos_il, sin_a, sin_b)
    score_bound = (HEAD_DIM * BF16_NORM_SLACK * jnp.max(jnp.abs(q_gain)) * jnp.max(jnp.abs(k_gain)))
    bounded = (score_bound <= SCORE_BOUND_LOG2).astype(jnp.int32).reshape(1)
    og = _attention(bounded, qt, k, vt, sg0)
    x1 = _outproj_residual(og.reshape(t, ATTN_WIDTH), attn_w_out[0].astype(BF16), x2d, gate[0])

    cs, m1, t12 = _dft_tables()
    z, sg = _fourier_inproj(x1, norm_g[1][None, :], scale[1], shift[1],
                            fourier_w_in[0].astype(BF16), cs)
    fgate = _seq_dft_gate(z, sg, m1, t12)
    out = _fourier_outproj_final(fgate.reshape(t, d), fourier_w_out[0].astype(BF16), x1,
                                 gate[1], final_g[None, :])
    return out.reshape(b, s, d)
```

```python
import math

import numpy as np
import jax
import jax.numpy as jnp
from jax import lax
from jax.experimental import pallas as pl
from jax.experimental.pallas import tpu as pltpu

D_MODEL = 2048
BATCH = 4
SEQ = 8192
GRID_W = 64
HEAD_DIM = 128
N_HEADS = 16
N_KV_HEADS = 4
GQA_GROUP = N_HEADS // N_KV_HEADS
ATTN_WIDTH = N_HEADS * HEAD_DIM
KV_WIDTH = N_KV_HEADS * HEAD_DIM
ATTN_IN_WIDTH = 2 * ATTN_WIDTH + 2 * KV_WIDTH
ROPE_AXIS_DIM = HEAD_DIM // 2
ROPE_THETA = 10000.0
FOURIER_GROUPS = 8
FOURIER_GROUP_W = D_MODEL // FOURIER_GROUPS
EPS = 1e-6

DFT_N1 = 128
DFT_N2 = SEQ // DFT_N1

F32 = jnp.float32
BF16 = jnp.bfloat16

SCORE_BOUND_LOG2 = 64.0
BF16_NORM_SLACK = 1.01

VMEM_LIMIT_BYTES = 56 * 1024 * 1024

ROW_TILE = 512
ATTN_ROW_SUBTILE = 256
FOURIER_ROW_SUBTILE = 512
COL_TILE = 512
ADALN_COL_TILE = 1024
Q_TILE = ROW_TILE
KV_TILE = 256
BOUNDED_CHUNKS_PER_TRIP = 32
ONLINE_CHUNKS_PER_TRIP = 4
DFT_CH_TILE = 128
DFT_N2_PITCH = DFT_N2 + 8
DFT_N1_PITCH = DFT_N1 + 8
DFT_WIDEN_UNROLL = 8
DFT_STAGE1_UNROLL = 16
DFT_STAGE2_UNROLL = 32
DFT_GATE_UNROLL = 4


def _params(*semantics):
    return pltpu.CompilerParams(dimension_semantics=semantics,
                                vmem_limit_bytes=VMEM_LIMIT_BYTES)


def _silu(x):
    return x * jax.nn.sigmoid(x)


def _adaln_body(ct_ref, w_ref, b_ref, o_ref):
    w = w_ref[0]
    c_act = _silu(ct_ref[...])
    for bi in range(ct_ref.shape[1]):
        o_ref[0, bi:bi + 1, :] = (jnp.sum(w * c_act[:, bi:bi + 1], axis=0, keepdims=True)
                                  + b_ref[0])


def _adaln(c_t, ada_w, ada_b):
    depth, d, n = ada_w.shape
    b = c_t.shape[1]
    return pl.pallas_call(
        _adaln_body,
        grid=(depth, n // ADALN_COL_TILE),
        in_specs=[
            pl.BlockSpec((d, b), lambda l, j: (0, 0)),
            pl.BlockSpec((1, d, ADALN_COL_TILE), lambda l, j: (l, 0, j)),
            pl.BlockSpec((1, 1, ADALN_COL_TILE), lambda l, j: (l, 0, j)),
        ],
        out_specs=pl.BlockSpec((1, b, ADALN_COL_TILE), lambda l, j: (l, 0, j)),
        out_shape=jax.ShapeDtypeStruct((depth, b, n), F32),
        compiler_params=_params("arbitrary", "arbitrary"),
        name="adaln",
    )(c_t, ada_w, ada_b.reshape(depth, 1, n))


def _modulated_norm(x_ref, g_ref, sc_ref, sh_ref, rows):
    x = x_ref[rows, :]
    ms = jnp.mean(x * x, axis=-1, keepdims=True)
    a = g_ref[...] * (1.0 + sc_ref[0])
    return (x * lax.rsqrt(ms + EPS) * a + sh_ref[0]).astype(BF16)


def _row_subtiles(rows_per_matmul):
    return [slice(r, r + rows_per_matmul) for r in range(0, ROW_TILE, rows_per_matmul)]


def _attn_inproj_body(x_ref, g_ref, sc_ref, sh_ref, w_ref, gain_ref, cos_ref, sina_ref, sinb_ref,
                      qt_ref, k_ref, vt_ref, sg_ref):
    heads_per_tile = COL_TILE // HEAD_DIM
    n_q_tiles = ATTN_WIDTH // COL_TILE
    n_qk_tiles = (ATTN_WIDTH + KV_WIDTH) // COL_TILE
    for rows in _row_subtiles(ATTN_ROW_SUBTILE):
        h = _modulated_norm(x_ref, g_ref, sc_ref, sh_ref, rows)
        cos = cos_ref[rows, :]
        sin_a = sina_ref[rows, :]
        sin_b = sinb_ref[rows, :]
        for j in range(ATTN_IN_WIDTH // COL_TILE):
            acc = jnp.dot(h, w_ref[:, j * COL_TILE:(j + 1) * COL_TILE],
                          preferred_element_type=F32)
            for hh in range(heads_per_tile):
                xh = acc[:, hh * HEAD_DIM:(hh + 1) * HEAD_DIM]
                if j < n_qk_tiles:
                    col0 = j * COL_TILE + hh * HEAD_DIM
                    ms = jnp.mean(xh * xh, axis=-1, keepdims=True)
                    xn = xh * lax.rsqrt(ms + EPS) * gain_ref[:, col0:col0 + HEAD_DIM]
                    rot = (xn * cos + pltpu.roll(xn, HEAD_DIM - 1, axis=1) * sin_a
                           + pltpu.roll(xn, 1, axis=1) * sin_b)
                    if j < n_q_tiles:
                        lanes = slice(hh * ROW_TILE + rows.start, hh * ROW_TILE + rows.stop)
                        qt_ref[0, j, 0, :, lanes] = rot.T.astype(BF16)
                    else:
                        k_ref[0, hh, rows, :] = rot.astype(BF16)
                elif j == n_qk_tiles:
                    vt_ref[0, hh, :, rows] = xh.T.astype(BF16)
            if j > n_qk_tiles:
                g0 = (j - n_qk_tiles - 1) * COL_TILE
                sg_ref[0, rows, g0:g0 + COL_TILE] = _silu(acc).astype(BF16)


def _resident(shape):
    return pl.BlockSpec(shape, lambda *_: (0,) * len(shape), pipeline_mode=pl.Buffered(1))


def _attn_inproj(x2d, norm_g, scale, shift, w_bf16, gain_vec, cos_il, sin_a, sin_b):
    t, d = x2d.shape
    n = w_bf16.shape[1]
    tpb = SEQ // ROW_TILE
    rope_spec = pl.BlockSpec((ROW_TILE, HEAD_DIM), lambda i: (i % tpb, 0))
    return pl.pallas_call(
        _attn_inproj_body,
        grid=(t // ROW_TILE,),
        in_specs=[
            pl.BlockSpec((ROW_TILE, d), lambda i: (i, 0)),
            _resident((1, d)),
            pl.BlockSpec((1, 1, d), lambda i: (i // tpb, 0, 0)),
            pl.BlockSpec((1, 1, d), lambda i: (i // tpb, 0, 0)),
            _resident((d, n)),
            _resident(gain_vec.shape),
            rope_spec, rope_spec, rope_spec,
        ],
        out_specs=[
            pl.BlockSpec((1, N_KV_HEADS, 1, HEAD_DIM, GQA_GROUP * ROW_TILE),
                         lambda i: (i // tpb, 0, i % tpb, 0, 0)),
            pl.BlockSpec((1, N_KV_HEADS, ROW_TILE, HEAD_DIM), lambda i: (i // tpb, 0, i % tpb, 0)),
            pl.BlockSpec((1, N_KV_HEADS, HEAD_DIM, ROW_TILE), lambda i: (i // tpb, 0, 0, i % tpb)),
            pl.BlockSpec((1, ROW_TILE, ATTN_WIDTH), lambda i: (i // tpb, i % tpb, 0)),
        ],
        out_shape=[
            jax.ShapeDtypeStruct((BATCH, N_KV_HEADS, tpb, HEAD_DIM, GQA_GROUP * ROW_TILE), BF16),
            jax.ShapeDtypeStruct((BATCH, N_KV_HEADS, SEQ, HEAD_DIM), BF16),
            jax.ShapeDtypeStruct((BATCH, N_KV_HEADS, HEAD_DIM, SEQ), BF16),
            jax.ShapeDtypeStruct((BATCH, SEQ, ATTN_WIDTH), BF16),
        ],
        compiler_params=_params("arbitrary"),
        name="attn_inproj",
    )(x2d, norm_g, scale, shift, w_bf16, gain_vec, cos_il, sin_a, sin_b)


def _attn_body(bounded_ref, qt_ref, k_ref, vt_ref, sg_ref, o_ref, m_scr, l_scr, acc_scr, st_scr):
    n_kv_steps = SEQ // KV_TILE
    qt = qt_ref.at[0, 0, 0]

    def chunk(c):
        start = c * KV_TILE if isinstance(c, int) else pl.multiple_of(c * KV_TILE, KV_TILE)
        return pl.ds(start, KV_TILE)

    def scores(c):
        return jnp.dot(k_ref[0, 0, chunk(c), :], qt[...], preferred_element_type=F32)

    def weighted_values(p, c):
        return jnp.dot(vt_ref[0, 0, :, chunk(c)], p.astype(BF16), preferred_element_type=F32)

    def consume_online(st, c, first):
        del first
        m_old = m_scr[...]
        m_new = jnp.maximum(m_old, jnp.max(st, axis=0, keepdims=True))
        alpha = jnp.exp2(m_old - m_new)
        p = jnp.exp2(st - m_new)
        l_scr[0:1] = alpha * l_scr[0:1] + jnp.sum(p, axis=0, keepdims=True)
        acc_scr[...] = alpha * acc_scr[...] + weighted_values(p, c)
        m_scr[...] = m_new

    def consume_bounded(st, c, first):
        p = jnp.exp2(st)
        part = jnp.sum(p.reshape(KV_TILE // 8, 8, p.shape[1]), axis=0)
        pv = weighted_values(p, c)
        if first:
            l_scr[...] = part
            acc_scr[...] = pv
        else:
            l_scr[...] += part
            acc_scr[...] += pv

    def run(consume, chunks_per_trip):
        if chunks_per_trip == n_kv_steps:
            st = scores(0)
            for j in range(n_kv_steps):
                st_next = scores(j + 1) if j + 1 < n_kv_steps else None
                consume(st, j, j == 0)
                st = st_next
            return

        m_scr[...] = jnp.full(m_scr.shape, -jnp.inf, F32)
        l_scr[...] = jnp.zeros(l_scr.shape, F32)
        acc_scr[...] = jnp.zeros(acc_scr.shape, F32)
        st_scr[...] = scores(0)

        def step(i, carry):
            first = i * chunks_per_trip
            st = st_scr[...]
            for j in range(chunks_per_trip):
                nxt = jnp.minimum(first + j + 1, n_kv_steps - 1)
                st_next = scores(nxt)
                consume(st, first + j, False)
                st = st_next
            st_scr[...] = st
            return carry

        lax.fori_loop(0, n_kv_steps // chunks_per_trip, step, 0)

    @pl.when(bounded_ref[0] != 0)
    def _():
        run(consume_bounded, BOUNDED_CHUNKS_PER_TRIP)

    @pl.when(bounded_ref[0] == 0)
    def _():
        run(consume_online, ONLINE_CHUNKS_PER_TRIP)

    ot = (acc_scr[...] * (1.0 / jnp.sum(l_scr[...], axis=0, keepdims=True))).astype(BF16)
    for hh in range(GQA_GROUP):
        sl = slice(hh * HEAD_DIM, (hh + 1) * HEAD_DIM)
        o = ot[:, hh * Q_TILE:(hh + 1) * Q_TILE].T
        o_ref[0, :, sl] = o * sg_ref[0, :, sl]


def _attention(bounded, qt, k, vt, sg):
    b, _, s, _ = k.shape
    group_w = GQA_GROUP * HEAD_DIM
    nq = GQA_GROUP * Q_TILE
    return pl.pallas_call(
        _attn_body,
        grid=(b, N_KV_HEADS, s // Q_TILE),
        in_specs=[
            pl.BlockSpec(memory_space=pltpu.SMEM),
            pl.BlockSpec((1, 1, 1, HEAD_DIM, nq), lambda bi, kh, qi: (bi, kh, qi, 0, 0)),
            pl.BlockSpec((1, 1, s, HEAD_DIM), lambda bi, kh, qi: (bi, kh, 0, 0)),
            pl.BlockSpec((1, 1, HEAD_DIM, s), lambda bi, kh, qi: (bi, kh, 0, 0)),
            pl.BlockSpec((1, Q_TILE, group_w), lambda bi, kh, qi: (bi, qi, kh)),
        ],
        out_specs=pl.BlockSpec((1, Q_TILE, group_w), lambda bi, kh, qi: (bi, qi, kh)),
        out_shape=jax.ShapeDtypeStruct((b, s, ATTN_WIDTH), BF16),
        scratch_shapes=[
            pltpu.VMEM((1, nq), F32),
            pltpu.VMEM((8, nq), F32),
            pltpu.VMEM((HEAD_DIM, nq), F32),
            pltpu.VMEM((KV_TILE, nq), F32),
        ],
        compiler_params=_params("arbitrary", "arbitrary", "arbitrary"),
        name="attention",
    )(bounded, qt, k, vt, sg)


def _outproj_body(a_ref, w_ref, x_ref, gm_ref, o_ref):
    y = jnp.dot(a_ref[...], w_ref[...], preferred_element_type=F32)
    o_ref[...] = x_ref[...] + gm_ref[0] * y


def _outproj_residual(a2d, w_bf16, x2d, gate_mod):
    t, d = x2d.shape
    tiles_per_batch = SEQ // ROW_TILE
    return pl.pallas_call(
        _outproj_body,
        grid=(t // ROW_TILE,),
        in_specs=[
            pl.BlockSpec((ROW_TILE, a2d.shape[1]), lambda i: (i, 0)),
            pl.BlockSpec(w_bf16.shape, lambda i: (0, 0)),
            pl.BlockSpec((ROW_TILE, d), lambda i: (i, 0)),
            pl.BlockSpec((1, 1, d), lambda i: (i // tiles_per_batch, 0, 0)),
        ],
        out_specs=pl.BlockSpec((ROW_TILE, d), lambda i: (i, 0)),
        out_shape=jax.ShapeDtypeStruct((t, d), F32),
        compiler_params=_params("arbitrary"),
        name="attn_outproj",
    )(a2d, w_bf16, x2d, gate_mod)


def _fourier_inproj_body(x_ref, g_ref, sc_ref, sh_ref, w_ref, cs_ref, z_ref, sg_ref):
    n_u_tiles = D_MODEL // COL_TILE
    for rows in _row_subtiles(FOURIER_ROW_SUBTILE):
        h = _modulated_norm(x_ref, g_ref, sc_ref, sh_ref, rows)
        for j in range(2 * n_u_tiles):
            acc = jnp.dot(h, w_ref[:, j * COL_TILE:(j + 1) * COL_TILE],
                          preferred_element_type=F32)
            slab0 = (j % n_u_tiles) * (COL_TILE // DFT_CH_TILE)
            if j < n_u_tiles:
                for gg in range(COL_TILE // FOURIER_GROUP_W):
                    u = acc[:, gg * FOURIER_GROUP_W:(gg + 1) * FOURIER_GROUP_W].astype(BF16)
                    z = jnp.dot(u, cs_ref[...], preferred_element_type=F32).astype(BF16)
                    for part in range(2 * FOURIER_GROUP_W // DFT_CH_TILE):
                        re_im, half = divmod(part, FOURIER_GROUP_W // DFT_CH_TILE)
                        slab = slab0 + gg * (FOURIER_GROUP_W // DFT_CH_TILE) + half
                        z_ref[0, slab, re_im, rows, :] = (
                            z[:, part * DFT_CH_TILE:(part + 1) * DFT_CH_TILE])
            else:
                sg = _silu(acc).astype(BF16)
                for part in range(COL_TILE // DFT_CH_TILE):
                    sg_ref[0, slab0 + part, rows, :] = (
                        sg[:, part * DFT_CH_TILE:(part + 1) * DFT_CH_TILE])


def _fourier_inproj(x2d, norm_g, scale, shift, w_bf16, cs):
    t, d = x2d.shape
    tiles_per_batch = SEQ // ROW_TILE
    n_slabs = D_MODEL // DFT_CH_TILE
    return pl.pallas_call(
        _fourier_inproj_body,
        grid=(t // ROW_TILE,),
        in_specs=[
            pl.BlockSpec((ROW_TILE, d), lambda i: (i, 0)),
            _resident((1, d)),
            pl.BlockSpec((1, 1, d), lambda i: (i // tiles_per_batch, 0, 0)),
            pl.BlockSpec((1, 1, d), lambda i: (i // tiles_per_batch, 0, 0)),
            _resident(w_bf16.shape),
            _resident(cs.shape),
        ],
        out_specs=[
            pl.BlockSpec((1, n_slabs, 2, ROW_TILE, DFT_CH_TILE),
                         lambda i: (i // tiles_per_batch, 0, 0, i % tiles_per_batch, 0)),
            pl.BlockSpec((1, n_slabs, ROW_TILE, DFT_CH_TILE),
                         lambda i: (i // tiles_per_batch, 0, i % tiles_per_batch, 0)),
        ],
        out_shape=[jax.ShapeDtypeStruct((BATCH, n_slabs, 2, SEQ, DFT_CH_TILE), BF16),
                   jax.ShapeDtypeStruct((BATCH, n_slabs, SEQ, DFT_CH_TILE), BF16)],
        compiler_params=_params("arbitrary"),
        name="fourier_inproj",
    )(x2d, norm_g, scale, shift, w_bf16, cs)


def _seq_dft_body(z_ref, sg_ref, m_ref, t_ref, o_ref, zf_scr, a_scr, f_scr):
    c = DFT_CH_TILE
    im_rows = DFT_N1 * DFT_N2_PITCH

    def widen(i, carry):
        for u in range(DFT_WIDEN_UNROLL):
            n1 = i * DFT_WIDEN_UNROLL + u
            src = pl.multiple_of(n1 * DFT_N2, DFT_N2)
            dst = pl.multiple_of(n1 * DFT_N2_PITCH, 8)
            zf_scr[pl.ds(dst, DFT_N2), :] = z_ref[0, 0, 0, pl.ds(src, DFT_N2), :].astype(F32)
            zf_scr[pl.ds(im_rows + dst, DFT_N2), :] = (
                z_ref[0, 0, 1, pl.ds(src, DFT_N2), :].astype(F32))
        return carry

    lax.fori_loop(0, DFT_N1 // DFT_WIDEN_UNROLL, widen, 0)

    def stage1(i, carry):
        for u in range(DFT_STAGE1_UNROLL):
            n2 = (i * DFT_STAGE1_UNROLL + u) * 2
            halves = []
            for dn in range(2):
                xr = zf_scr[pl.ds(n2 + dn, DFT_N1, stride=DFT_N2_PITCH), :]
                xi = zf_scr[pl.ds(im_rows + n2 + dn, DFT_N1, stride=DFT_N2_PITCH), :]
                halves.append(jnp.concatenate([xr, xi], axis=0))
            x = jnp.concatenate(halves, axis=1).astype(BF16)
            a = jnp.dot(m_ref[...], x, preferred_element_type=F32)
            for dn in range(2):
                a_scr[pl.ds(n2 + dn, 2 * DFT_N1, stride=DFT_N2_PITCH), :] = (
                    a[:, dn * c:(dn + 1) * c])
        return carry

    lax.fori_loop(0, DFT_N2 // (2 * DFT_STAGE1_UNROLL), stage1, 0)

    def stage2(i, carry):
        for u in range(DFT_STAGE2_UNROLL):
            k1 = i * DFT_STAGE2_UNROLL + u
            r = pl.multiple_of(k1 * 2 * DFT_N2_PITCH, 8)
            rows = jnp.concatenate([a_scr[pl.ds(r, DFT_N2), :],
                                    a_scr[pl.ds(r + DFT_N2_PITCH, DFT_N2), :]],
                                   axis=0).astype(BF16)
            f = jnp.dot(t_ref[k1], rows, preferred_element_type=F32)
            f_scr[pl.ds(k1, DFT_N2, stride=DFT_N1_PITCH), :] = f
        return carry

    lax.fori_loop(0, DFT_N1 // DFT_STAGE2_UNROLL, stage2, 0)

    def gate(i, carry):
        for u in range(DFT_GATE_UNROLL):
            k2 = i * DFT_GATE_UNROLL + u
            src = pl.multiple_of(k2 * DFT_N1_PITCH, 8)
            dst = pl.multiple_of(k2 * DFT_N1, DFT_N1)
            o_ref[0, pl.ds(dst, DFT_N1), :] = (
                f_scr[pl.ds(src, DFT_N1), :]
                * sg_ref[0, 0, pl.ds(dst, DFT_N1), :].astype(F32)).astype(BF16)
        return carry

    lax.fori_loop(0, DFT_N2 // DFT_GATE_UNROLL, gate, 0)


def _seq_dft_gate(z, sg, m1, t12):
    b, n_slabs, _, s, c = z.shape
    d = n_slabs * c
    return pl.pallas_call(
        _seq_dft_body,
        grid=(b, d // c),
        in_specs=[
            pl.BlockSpec((1, 1, 2, s, c), lambda bi, ci: (bi, ci, 0, 0, 0)),
            pl.BlockSpec((1, 1, s, c), lambda bi, ci: (bi, ci, 0, 0)),
            _resident(m1.shape),
            _resident(t12.shape),
        ],
        out_specs=pl.BlockSpec((1, s, c), lambda bi, ci: (bi, 0, ci)),
        out_shape=jax.ShapeDtypeStruct((b, s, d), BF16),
        scratch_shapes=[
            pltpu.VMEM((2 * DFT_N1 * DFT_N2_PITCH, c), F32),
            pltpu.VMEM((2 * DFT_N1 * DFT_N2_PITCH, c), F32),
            pltpu.VMEM((DFT_N2 * DFT_N1_PITCH, c), F32),
        ],
        compiler_params=_params("arbitrary", "arbitrary"),
        name="seq_dft_gate",
    )(z, sg, m1, t12)


def _final_body(a_ref, w_ref, x_ref, gm_ref, fg_ref, o_ref):
    y = jnp.dot(a_ref[...], w_ref[...], preferred_element_type=F32)
    x2 = x_ref[...] + gm_ref[0] * y
    ms = jnp.mean(x2 * x2, axis=-1, keepdims=True)
    o_ref[...] = x2 * lax.rsqrt(ms + EPS) * fg_ref[...]


def _fourier_outproj_final(a2d, w_bf16, x2d, gate_mod, final_g):
    t, d = x2d.shape
    tiles_per_batch = SEQ // ROW_TILE
    return pl.pallas_call(
        _final_body,
        grid=(t // ROW_TILE,),
        in_specs=[
            pl.BlockSpec((ROW_TILE, a2d.shape[1]), lambda i: (i, 0)),
            _resident(w_bf16.shape),
            pl.BlockSpec((ROW_TILE, d), lambda i: (i, 0)),
            pl.BlockSpec((1, 1, d), lambda i: (i // tiles_per_batch, 0, 0)),
            _resident((1, d)),
        ],
        out_specs=pl.BlockSpec((ROW_TILE, d), lambda i: (i, 0)),
        out_shape=jax.ShapeDtypeStruct((t, d), F32),
        compiler_params=_params("arbitrary"),
        name="fourier_outproj_final",
    )(a2d, w_bf16, x2d, gate_mod, final_g)


def _rope_tables():
    rows = SEQ // GRID_W
    inv_freq = ROPE_THETA ** (-jnp.arange(0, ROPE_AXIS_DIM, 2, dtype=F32) / ROPE_AXIS_DIM)
    n_freq = inv_freq.shape[0]
    pair = np.arange(HEAD_DIM) // 2
    lane_freq = inv_freq[pair % n_freq][None, :]
    uses_row = jnp.asarray(pair < n_freq)[None, None, :]
    even_lane = jnp.asarray(np.arange(HEAD_DIM) % 2 == 0)[None, None, :]
    row_ang = jnp.arange(rows).astype(F32)[:, None] * lane_freq
    col_ang = jnp.arange(GRID_W).astype(F32)[:, None] * lane_freq

    def per_position(fn):
        return jnp.where(uses_row, fn(row_ang)[:, None, :], fn(col_ang)[None, :, :])

    cos, sin = per_position(jnp.cos), per_position(jnp.sin)
    flat = lambda a: a.reshape(SEQ, HEAD_DIM)
    return (flat(cos), flat(jnp.where(even_lane, -sin, 0.0)), flat(jnp.where(even_lane, 0.0, sin)))


def _dft_tables():
    w = np.arange(FOURIER_GROUP_W)
    ang_w = 2.0 * np.pi * (np.outer(w, w) % FOURIER_GROUP_W) / FOURIER_GROUP_W
    cs = np.concatenate([np.cos(ang_w), -np.sin(ang_w)], axis=1) * 2.0 ** -4

    n1 = np.arange(DFT_N1)
    ang1 = 2.0 * np.pi * (np.outer(n1, n1) % DFT_N1) / DFT_N1
    c1, s1 = np.cos(ang1), np.sin(ang1)
    m1 = np.stack([np.concatenate([c1, s1], axis=1),
                   np.concatenate([-s1, c1], axis=1)], axis=1).reshape(2 * DFT_N1, 2 * DFT_N1)
    m1 = m1 * 2.0 ** -3

    k = n1[:, None] + DFT_N1 * np.arange(DFT_N2)[None, :]
    n2 = np.arange(DFT_N2)
    ang2 = 2.0 * np.pi * ((k[:, :, None] * n2[None, None, :]) % SEQ) / SEQ
    t12 = np.concatenate([np.cos(ang2), np.sin(ang2)], axis=2) * 2.0 ** -3.5
    as_bf16 = lambda a: jnp.asarray(a, dtype=F32).astype(BF16)
    return as_bf16(cs), as_bf16(m1), as_bf16(t12)


def kernel(x, c, norm_g, ada_w, ada_b, attn_w_in, attn_q_gain, attn_k_gain, attn_w_out,
           fourier_w_in, fourier_w_out, final_g):
    b, s, d = x.shape
    t = b * s

    mod = _adaln(c.T, ada_w, ada_b)
    shift = mod[:, :, None, :d]
    scale = mod[:, :, None, d:2 * d]
    gate = mod[:, :, None, 2 * d:]

    q_gain = attn_q_gain[0] * (math.log2(math.e) / math.sqrt(HEAD_DIM))
    k_gain = attn_k_gain[0]
    gain_vec = jnp.concatenate([jnp.tile(q_gain, N_HEADS), jnp.tile(k_gain, N_KV_HEADS)])[None, :]
    cos_il, sin_a, sin_b = _rope_tables()

    x2d = x.reshape(t, d)
    qt, k, vt, sg0 = _attn_inproj(x2d, norm_g[0][None, :], scale[0], shift[0],
                                  attn_w_in[0].astype(BF16), gain_vec, cos_il, sin_a, sin_b)
    score_bound = (HEAD_DIM * BF16_NORM_SLACK * jnp.max(jnp.abs(q_gain)) * jnp.max(jnp.abs(k_gain)))
    bounded = (score_bound <= SCORE_BOUND_LOG2).astype(jnp.int32).reshape(1)
    og = _attention(bounded, qt, k, vt, sg0)
    x1 = _outproj_residual(og.reshape(t, ATTN_WIDTH), attn_w_out[0].astype(BF16), x2d, gate[0])

    cs, m1, t12 = _dft_tables()
    z, sg = _fourier_inproj(x1, norm_g[1][None, :], scale[1], shift[1],
                            fourier_w_in[0].astype(BF16), cs)
    fgate = _seq_dft_gate(z, sg, m1, t12)
    out = _fourier_outproj_final(fgate.reshape(t, d), fourier_w_out[0].astype(BF16), x1,
                                 gate[1], final_g[None, :])
    return out.reshape(b, s, d)
```

```python
import math

import numpy as np
import jax
import jax.numpy as jnp
from jax import lax
from jax.experimental import pallas as pl
from jax.experimental.pallas import tpu as pltpu

D_MODEL = 2048
BATCH = 4
SEQ = 8192
GRID_W = 64
HEAD_DIM = 128
N_HEADS = 16
N_KV_HEADS = 4
GQA_GROUP = N_HEADS // N_KV_HEADS
ATTN_WIDTH = N_HEADS * HEAD_DIM
KV_WIDTH = N_KV_HEADS * HEAD_DIM
ATTN_IN_WIDTH = 2 * ATTN_WIDTH + 2 * KV_WIDTH
ROPE_AXIS_DIM = HEAD_DIM // 2
ROPE_THETA = 10000.0
FOURIER_GROUPS = 8
FOURIER_GROUP_W = D_MODEL // FOURIER_GROUPS
EPS = 1e-6

DFT_N1 = 128
DFT_N2 = SEQ // DFT_N1

F32 = jnp.float32
BF16 = jnp.bfloat16

SCORE_BOUND_LOG2 = 64.0
BF16_NORM_SLACK = 1.01

VMEM_LIMIT_BYTES = 56 * 1024 * 1024

ROW_TILE = 512
ATTN_ROW_SUBTILE = 256
FOURIER_ROW_SUBTILE = 512
COL_TILE = 512
ADALN_COL_TILE = 1024
Q_TILE = ROW_TILE
KV_TILE = 256
BOUNDED_CHUNKS_PER_TRIP = 32
ONLINE_CHUNKS_PER_TRIP = 4
DFT_CH_TILE = 128
DFT_N2_PITCH = DFT_N2 + 8
DFT_N1_PITCH = DFT_N1 + 8
DFT_WIDEN_UNROLL = 8
DFT_STAGE1_UNROLL = 16
DFT_STAGE2_UNROLL = 32
DFT_GATE_UNROLL = 4


def _params(*semantics):
    return pltpu.CompilerParams(dimension_semantics=semantics,
                                vmem_limit_bytes=VMEM_LIMIT_BYTES)


def _silu(x):
    return x * jax.nn.sigmoid(x)


def _adaln_body(ct_ref, w_ref, b_ref, o_ref):
    w = w_ref[0]
    c_act = _silu(ct_ref[...])
    for bi in range(ct_ref.shape[1]):
        o_ref[0, bi:bi + 1, :] = (jnp.sum(w * c_act[:, bi:bi + 1], axis=0, keepdims=True)
                                  + b_ref[0])


def _adaln(c_t, ada_w, ada_b):
    depth, d, n = ada_w.shape
    b = c_t.shape[1]
    return pl.pallas_call(
        _adaln_body,
        grid=(depth, n // ADALN_COL_TILE),
        in_specs=[
            pl.BlockSpec((d, b), lambda l, j: (0, 0)),
            pl.BlockSpec((1, d, ADALN_COL_TILE), lambda l, j: (l, 0, j)),
            pl.BlockSpec((1, 1, ADALN_COL_TILE), lambda l, j: (l, 0, j)),
        ],
        out_specs=pl.BlockSpec((1, b, ADALN_COL_TILE), lambda l, j: (l, 0, j)),
        out_shape=jax.ShapeDtypeStruct((depth, b, n), F32),
        compiler_params=_params("arbitrary", "arbitrary"),
        name="adaln",
    )(c_t, ada_w, ada_b.reshape(depth, 1, n))


def _modulated_norm(x_ref, g_ref, sc_ref, sh_ref, rows):
    x = x_ref[rows, :]
    ms = jnp.mean(x * x, axis=-1, keepdims=True)
    a = g_ref[...] * (1.0 + sc_ref[0])
    return (x * lax.rsqrt(ms + EPS) * a + sh_ref[0]).astype(BF16)


def _row_subtiles(rows_per_matmul):
    return [slice(r, r + rows_per_matmul) for r in range(0, ROW_TILE, rows_per_matmul)]


def _attn_inproj_body(x_ref, g_ref, sc_ref, sh_ref, w_ref, gain_ref, cos_ref, sina_ref, sinb_ref,
                      qt_ref, k_ref, vt_ref, sg_ref):
    heads_per_tile = COL_TILE // HEAD_DIM
    n_q_tiles = ATTN_WIDTH // COL_TILE
    n_qk_tiles = (ATTN_WIDTH + KV_WIDTH) // COL_TILE
    for rows in _row_subtiles(ATTN_ROW_SUBTILE):
        h = _modulated_norm(x_ref, g_ref, sc_ref, sh_ref, rows)
        cos = cos_ref[rows, :]
        sin_a = sina_ref[rows, :]
        sin_b = sinb_ref[rows, :]
        for j in range(ATTN_IN_WIDTH // COL_TILE):
            acc = jnp.dot(h, w_ref[:, j * COL_TILE:(j + 1) * COL_TILE],
                          preferred_element_type=F32)
            for hh in range(heads_per_tile):
                xh = acc[:, hh * HEAD_DIM:(hh + 1) * HEAD_DIM]
                if j < n_qk_tiles:
                    col0 = j * COL_TILE + hh * HEAD_DIM
                    ms = jnp.mean(xh * xh, axis=-1, keepdims=True)
                    xn = xh * lax.rsqrt(ms + EPS) * gain_ref[:, col0:col0 + HEAD_DIM]
                    rot = (xn * cos + pltpu.roll(xn, HEAD_DIM - 1, axis=1) * sin_a
                           + pltpu.roll(xn, 1, axis=1) * sin_b)
                    if j < n_q_tiles:
                        lanes = slice(hh * ROW_TILE + rows.start, hh * ROW_TILE + rows.stop)
                        qt_ref[0, j, 0, :, lanes] = rot.T.astype(BF16)
                    else:
                        k_ref[0, hh, rows, :] = rot.astype(BF16)
                elif j == n_qk_tiles:
                    vt_ref[0, hh, :, rows] = xh.T.astype(BF16)
            if j > n_qk_tiles:
                g0 = (j - n_qk_tiles - 1) * COL_TILE
                sg_ref[0, rows, g0:g0 + COL_TILE] = _silu(acc).astype(BF16)


def _resident(shape):
    return pl.BlockSpec(shape, lambda *_: (0,) * len(shape), pipeline_mode=pl.Buffered(1))


def _attn_inproj(x2d, norm_g, scale, shift, w_bf16, gain_vec, cos_il, sin_a, sin_b):
    t, d = x2d.shape
    n = w_bf16.shape[1]
    tpb = SEQ // ROW_TILE
    rope_spec = pl.BlockSpec((ROW_TILE, HEAD_DIM), lambda i: (i % tpb, 0))
    return pl.pallas_call(
        _attn_inproj_body,
        grid=(t // ROW_TILE,),
        in_specs=[
            pl.BlockSpec((ROW_TILE, d), lambda i: (i, 0)),
            _resident((1, d)),
            pl.BlockSpec((1, 1, d), lambda i: (i // tpb, 0, 0)),
            pl.BlockSpec((1, 1, d), lambda i: (i // tpb, 0, 0)),
            _resident((d, n)),
            _resident(gain_vec.shape),
            rope_spec, rope_spec, rope_spec,
        ],
        out_specs=[
            pl.BlockSpec((1, N_KV_HEADS, 1, HEAD_DIM, GQA_GROUP * ROW_TILE),
                         lambda i: (i // tpb, 0, i % tpb, 0, 0)),
            pl.BlockSpec((1, N_KV_HEADS, ROW_TILE, HEAD_DIM), lambda i: (i // tpb, 0, i % tpb, 0)),
            pl.BlockSpec((1, N_KV_HEADS, HEAD_DIM, ROW_TILE), lambda i: (i // tpb, 0, 0, i % tpb)),
            pl.BlockSpec((1, ROW_TILE, ATTN_WIDTH), lambda i: (i // tpb, i % tpb, 0)),
        ],
        out_shape=[
            jax.ShapeDtypeStruct((BATCH, N_KV_HEADS, tpb, HEAD_DIM, GQA_GROUP * ROW_TILE), BF16),
            jax.ShapeDtypeStruct((BATCH, N_KV_HEADS, SEQ, HEAD_DIM), BF16),
            jax.ShapeDtypeStruct((BATCH, N_KV_HEADS, HEAD_DIM, SEQ), BF16),
            jax.ShapeDtypeStruct((BATCH, SEQ, ATTN_WIDTH), BF16),
        ],
        compiler_params=_params("arbitrary"),
        name="attn_inproj",
    )(x2d, norm_g, scale, shift, w_bf16, gain_vec, cos_il, sin_a, sin_b)


def _attn_body(bounded_ref, qt_ref, k_ref, vt_ref, sg_ref, o_ref, m_scr, l_scr, acc_scr, st_scr):
    n_kv_steps = SEQ // KV_TILE
    qt = qt_ref.at[0, 0, 0]

    def chunk(c):
        start = c * KV_TILE if isinstance(c, int) else pl.multiple_of(c * KV_TILE, KV_TILE)
        return pl.ds(start, KV_TILE)

    def scores(c):
        return jnp.dot(k_ref[0, 0, chunk(c), :], qt[...], preferred_element_type=F32)

    def weighted_values(p, c):
        return jnp.dot(vt_ref[0, 0, :, chunk(c)], p.astype(BF16), preferred_element_type=F32)

    def consume_online(st, c, first):
        del first
        m_old = m_scr[...]
        m_new = jnp.maximum(m_old, jnp.max(st, axis=0, keepdims=True))
        alpha = jnp.exp2(m_old - m_new)
        p = jnp.exp2(st - m_new)
        l_scr[0:1] = alpha * l_scr[0:1] + jnp.sum(p, axis=0, keepdims=True)
        acc_scr[...] = alpha * acc_scr[...] + weighted_values(p, c)
        m_scr[...] = m_new

    def consume_bounded(st, c, first):
        p = jnp.exp2(st)
        part = jnp.sum(p.reshape(KV_TILE // 8, 8, p.shape[1]), axis=0)
        pv = weighted_values(p, c)
        if first:
            l_scr[...] = part
            acc_scr[...] = pv
        else:
            l_scr[...] += part
            acc_scr[...] += pv

    def run(consume, chunks_per_trip):
        if chunks_per_trip == n_kv_steps:
            st = scores(0)
            for j in range(n_kv_steps):
                st_next = scores(j + 1) if j + 1 < n_kv_steps else None
                consume(st, j, j == 0)
                st = st_next
            return

        m_scr[...] = jnp.full(m_scr.shape, -jnp.inf, F32)
        l_scr[...] = jnp.zeros(l_scr.shape, F32)
        acc_scr[...] = jnp.zeros(acc_scr.shape, F32)
        st_scr[...] = scores(0)

        def step(i, carry):
            first = i * chunks_per_trip
            st = st_scr[...]
            for j in range(chunks_per_trip):
                nxt = jnp.minimum(first + j + 1, n_kv_steps - 1)
                st_next = scores(nxt)
                consume(st, first + j, False)
                st = st_next
            st_scr[...] = st
            return carry

        lax.fori_loop(0, n_kv_steps // chunks_per_trip, step, 0)

    @pl.when(bounded_ref[0] != 0)
    def _():
        run(consume_bounded, BOUNDED_CHUNKS_PER_TRIP)

    @pl.when(bounded_ref[0] == 0)
    def _():
        run(consume_online, ONLINE_CHUNKS_PER_TRIP)

    ot = (acc_scr[...] * (1.0 / jnp.sum(l_scr[...], axis=0, keepdims=True))).astype(BF16)
    for hh in range(GQA_GROUP):
        sl = slice(hh * HEAD_DIM, (hh + 1) * HEAD_DIM)
        o = ot[:, hh * Q_TILE:(hh + 1) * Q_TILE].T
        o_ref[0, :, sl] = o * sg_ref[0, :, sl]


def _attention(bounded, qt, k, vt, sg):
    b, _, s, _ = k.shape
    group_w = GQA_GROUP * HEAD_DIM
    nq = GQA_GROUP * Q_TILE
    return pl.pallas_call(
        _attn_body,
        grid=(b, N_KV_HEADS, s // Q_TILE),
        in_specs=[
            pl.BlockSpec(memory_space=pltpu.SMEM),
            pl.BlockSpec((1, 1, 1, HEAD_DIM, nq), lambda bi, kh, qi: (bi, kh, qi, 0, 0)),
            pl.BlockSpec((1, 1, s, HEAD_DIM), lambda bi, kh, qi: (bi, kh, 0, 0)),
            pl.BlockSpec((1, 1, HEAD_DIM, s), lambda bi, kh, qi: (bi, kh, 0, 0)),
            pl.BlockSpec((1, Q_TILE, group_w), lambda bi, kh, qi: (bi, qi, kh)),
        ],
        out_specs=pl.BlockSpec((1, Q_TILE, group_w), lambda bi, kh, qi: (bi, qi, kh)),
        out_shape=jax.ShapeDtypeStruct((b, s, ATTN_WIDTH), BF16),
        scratch_shapes=[
            pltpu.VMEM((1, nq), F32),
            pltpu.VMEM((8, nq), F32),
            pltpu.VMEM((HEAD_DIM, nq), F32),
            pltpu.VMEM((KV_TILE, nq), F32),
        ],
        compiler_params=_params("arbitrary", "arbitrary", "arbitrary"),
        name="attention",
    )(bounded, qt, k, vt, sg)


def _outproj_body(a_ref, w_ref, x_ref, gm_ref, o_ref):
    for j in range(D_MODEL // COL_TILE):
        cols = slice(j * COL_TILE, (j + 1) * COL_TILE)
        y = jnp.dot(a_ref[...], w_ref[:, cols], preferred_element_type=F32)
        o_ref[:, cols] = x_ref[:, cols] + gm_ref[0, :, cols] * y


def _outproj_residual(a2d, w_bf16, x2d, gate_mod):
    t, d = x2d.shape
    tiles_per_batch = SEQ // ROW_TILE
    return pl.pallas_call(
        _outproj_body,
        grid=(t // ROW_TILE,),
        in_specs=[
            pl.BlockSpec((ROW_TILE, a2d.shape[1]), lambda i: (i, 0)),
            pl.BlockSpec(w_bf16.shape, lambda i: (0, 0)),
            pl.BlockSpec((ROW_TILE, d), lambda i: (i, 0)),
            pl.BlockSpec((1, 1, d), lambda i: (i // tiles_per_batch, 0, 0)),
        ],
        out_specs=pl.BlockSpec((ROW_TILE, d), lambda i: (i, 0)),
        out_shape=jax.ShapeDtypeStruct((t, d), F32),
        compiler_params=_params("arbitrary"),
        name="attn_outproj",
    )(a2d, w_bf16, x2d, gate_mod)


def _fourier_inproj_body(x_ref, g_ref, sc_ref, sh_ref, w_ref, cs_ref, z_ref, sg_ref):
    n_u_tiles = D_MODEL // COL_TILE
    for rows in _row_subtiles(FOURIER_ROW_SUBTILE):
        h = _modulated_norm(x_ref, g_ref, sc_ref, sh_ref, rows)
        for j in range(2 * n_u_tiles):
            acc = jnp.dot(h, w_ref[:, j * COL_TILE:(j + 1) * COL_TILE],
                          preferred_element_type=F32)
            if j < n_u_tiles:
                for gg in range(COL_TILE // FOURIER_GROUP_W):
                    sl = slice(j * COL_TILE + gg * FOURIER_GROUP_W,
                               j * COL_TILE + (gg + 1) * FOURIER_GROUP_W)
                    u = acc[:, gg * FOURIER_GROUP_W:(gg + 1) * FOURIER_GROUP_W].astype(BF16)
                    z = jnp.dot(u, cs_ref[...], preferred_element_type=F32)
                    z_ref[0, 0, rows, sl] = z[:, :FOURIER_GROUP_W].astype(BF16)
                    z_ref[0, 1, rows, sl] = z[:, FOURIER_GROUP_W:].astype(BF16)
            else:
                cols = slice((j - n_u_tiles) * COL_TILE, (j - n_u_tiles + 1) * COL_TILE)
                sg_ref[0, rows, cols] = _silu(acc).astype(BF16)


def _fourier_inproj(x2d, norm_g, scale, shift, w_bf16, cs):
    t, d = x2d.shape
    tiles_per_batch = SEQ // ROW_TILE
    return pl.pallas_call(
        _fourier_inproj_body,
        grid=(t // ROW_TILE,),
        in_specs=[
            pl.BlockSpec((ROW_TILE, d), lambda i: (i, 0)),
            _resident((1, d)),
            pl.BlockSpec((1, 1, d), lambda i: (i // tiles_per_batch, 0, 0)),
            pl.BlockSpec((1, 1, d), lambda i: (i // tiles_per_batch, 0, 0)),
            _resident(w_bf16.shape),
            _resident(cs.shape),
        ],
        out_specs=[
            pl.BlockSpec((1, 2, ROW_TILE, D_MODEL),
                         lambda i: (i // tiles_per_batch, 0, i % tiles_per_batch, 0)),
            pl.BlockSpec((1, ROW_TILE, D_MODEL),
                         lambda i: (i // tiles_per_batch, i % tiles_per_batch, 0)),
        ],
        out_shape=[jax.ShapeDtypeStruct((BATCH, 2, SEQ, D_MODEL), BF16),
                   jax.ShapeDtypeStruct((BATCH, SEQ, D_MODEL), BF16)],
        compiler_params=_params("arbitrary"),
        name="fourier_inproj",
    )(x2d, norm_g, scale, shift, w_bf16, cs)


def _seq_dft_body(z_ref, sg_ref, m_ref, t_ref, o_ref, zf_scr, a_scr, f_scr):
    c = DFT_CH_TILE
    im_rows = DFT_N1 * DFT_N2_PITCH

    def widen(i, carry):
        for u in range(DFT_WIDEN_UNROLL):
            n1 = i * DFT_WIDEN_UNROLL + u
            src = pl.multiple_of(n1 * DFT_N2, DFT_N2)
            dst = pl.multiple_of(n1 * DFT_N2_PITCH, 8)
            zf_scr[pl.ds(dst, DFT_N2), :] = z_ref[0, 0, pl.ds(src, DFT_N2), :].astype(F32)
            zf_scr[pl.ds(im_rows + dst, DFT_N2), :] = (
                z_ref[0, 1, pl.ds(src, DFT_N2), :].astype(F32))
        return carry

    lax.fori_loop(0, DFT_N1 // DFT_WIDEN_UNROLL, widen, 0)

    def stage1(i, carry):
        for u in range(DFT_STAGE1_UNROLL):
            n2 = (i * DFT_STAGE1_UNROLL + u) * 2
            halves = []
            for dn in range(2):
                xr = zf_scr[pl.ds(n2 + dn, DFT_N1, stride=DFT_N2_PITCH), :]
                xi = zf_scr[pl.ds(im_rows + n2 + dn, DFT_N1, stride=DFT_N2_PITCH), :]
                halves.append(jnp.concatenate([xr, xi], axis=0))
            x = jnp.concatenate(halves, axis=1).astype(BF16)
            a = jnp.dot(m_ref[...], x, preferred_element_type=F32)
            for dn in range(2):
                a_scr[pl.ds(n2 + dn, 2 * DFT_N1, stride=DFT_N2_PITCH), :] = (
                    a[:, dn * c:(dn + 1) * c])
        return carry

    lax.fori_loop(0, DFT_N2 // (2 * DFT_STAGE1_UNROLL), stage1, 0)

    def stage2(i, carry):
        for u in range(DFT_STAGE2_UNROLL):
            k1 = i * DFT_STAGE2_UNROLL + u
            r = pl.multiple_of(k1 * 2 * DFT_N2_PITCH, 8)
            rows = jnp.concatenate([a_scr[pl.ds(r, DFT_N2), :],
                                    a_scr[pl.ds(r + DFT_N2_PITCH, DFT_N2), :]],
                                   axis=0).astype(BF16)
            f = jnp.dot(t_ref[k1], rows, preferred_element_type=F32)
            f_scr[pl.ds(k1, DFT_N2, stride=DFT_N1_PITCH), :] = f
        return carry

    lax.fori_loop(0, DFT_N1 // DFT_STAGE2_UNROLL, stage2, 0)

    def gate(i, carry):
        for u in range(DFT_GATE_UNROLL):
            k2 = i * DFT_GATE_UNROLL + u
            src = pl.multiple_of(k2 * DFT_N1_PITCH, 8)
            dst = pl.multiple_of(k2 * DFT_N1, DFT_N1)
            o_ref[0, pl.ds(dst, DFT_N1), :] = (
                f_scr[pl.ds(src, DFT_N1), :]
                * sg_ref[0, pl.ds(dst, DFT_N1), :].astype(F32)).astype(BF16)
        return carry

    lax.fori_loop(0, DFT_N2 // DFT_GATE_UNROLL, gate, 0)


def _seq_dft_gate(z, sg, m1, t12):
    b, _, s, d = z.shape
    c = DFT_CH_TILE
    return pl.pallas_call(
        _seq_dft_body,
        grid=(b, d // c),
        in_specs=[
            pl.BlockSpec((1, 2, s, c), lambda bi, ci: (bi, 0, 0, ci)),
            pl.BlockSpec((1, s, c), lambda bi, ci: (bi, 0, ci)),
            _resident(m1.shape),
            _resident(t12.shape),
        ],
        out_specs=pl.BlockSpec((1, s, c), lambda bi, ci: (bi, 0, ci)),
        out_shape=jax.ShapeDtypeStruct((b, s, d), BF16),
        scratch_shapes=[
            pltpu.VMEM((2 * DFT_N1 * DFT_N2_PITCH, c), F32),
            pltpu.VMEM((2 * DFT_N1 * DFT_N2_PITCH, c), F32),
            pltpu.VMEM((DFT_N2 * DFT_N1_PITCH, c), F32),
        ],
        compiler_params=_params("arbitrary", "arbitrary"),
        name="seq_dft_gate",
    )(z, sg, m1, t12)


def _final_body(a_ref, w_ref, x_ref, gm_ref, fg_ref, o_ref):
    ss = jnp.zeros((ROW_TILE, 1), F32)
    for j in range(D_MODEL // COL_TILE):
        cols = slice(j * COL_TILE, (j + 1) * COL_TILE)
        y = jnp.dot(a_ref[...], w_ref[:, cols], preferred_element_type=F32)
        x2 = x_ref[:, cols] + gm_ref[0, :, cols] * y
        o_ref[:, cols] = x2
        ss = ss + jnp.sum(x2 * x2, axis=-1, keepdims=True)
    o_ref[...] = o_ref[...] * lax.rsqrt(ss * (1.0 / D_MODEL) + EPS) * fg_ref[...]


def _fourier_outproj_final(a2d, w_bf16, x2d, gate_mod, final_g):
    t, d = x2d.shape
    tiles_per_batch = SEQ // ROW_TILE
    return pl.pallas_call(
        _final_body,
        grid=(t // ROW_TILE,),
        in_specs=[
            pl.BlockSpec((ROW_TILE, a2d.shape[1]), lambda i: (i, 0)),
            _resident(w_bf16.shape),
            pl.BlockSpec((ROW_TILE, d), lambda i: (i, 0)),
            pl.BlockSpec((1, 1, d), lambda i: (i // tiles_per_batch, 0, 0)),
            _resident((1, d)),
        ],
        out_specs=pl.BlockSpec((ROW_TILE, d), lambda i: (i, 0)),
        out_shape=jax.ShapeDtypeStruct((t, d), F32),
        compiler_params=_params("arbitrary"),
        name="fourier_outproj_final",
    )(a2d, w_bf16, x2d, gate_mod, final_g)


def _rope_tables():
    rows = SEQ // GRID_W
    inv_freq = ROPE_THETA ** (-jnp.arange(0, ROPE_AXIS_DIM, 2, dtype=F32) / ROPE_AXIS_DIM)
    n_freq = inv_freq.shape[0]
    pair = np.arange(HEAD_DIM) // 2
    lane_freq = inv_freq[pair % n_freq][None, :]
    uses_row = jnp.asarray(pair < n_freq)[None, None, :]
    even_lane = jnp.asarray(np.arange(HEAD_DIM) % 2 == 0)[None, None, :]
    row_ang = jnp.arange(rows).astype(F32)[:, None] * lane_freq
    col_ang = jnp.arange(GRID_W).astype(F32)[:, None] * lane_freq

    def per_position(fn):
        return jnp.where(uses_row, fn(row_ang)[:, None, :], fn(col_ang)[None, :, :])

    cos, sin = per_position(jnp.cos), per_position(jnp.sin)
    flat = lambda a: a.reshape(SEQ, HEAD_DIM)
    return (flat(cos), flat(jnp.where(even_lane, -sin, 0.0)), flat(jnp.where(even_lane, 0.0, sin)))


def _dft_tables():
    w = np.arange(FOURIER_GROUP_W)
    ang_w = 2.0 * np.pi * (np.outer(w, w) % FOURIER_GROUP_W) / FOURIER_GROUP_W
    cs = np.concatenate([np.cos(ang_w), -np.sin(ang_w)], axis=1) * 2.0 ** -4

    n1 = np.arange(DFT_N1)
    ang1 = 2.0 * np.pi * (np.outer(n1, n1) % DFT_N1) / DFT_N1
    c1, s1 = np.cos(ang1), np.sin(ang1)
    m1 = np.stack([np.concatenate([c1, s1], axis=1),
                   np.concatenate([-s1, c1], axis=1)], axis=1).reshape(2 * DFT_N1, 2 * DFT_N1)
    m1 = m1 * 2.0 ** -3

    k = n1[:, None] + DFT_N1 * np.arange(DFT_N2)[None, :]
    n2 = np.arange(DFT_N2)
    ang2 = 2.0 * np.pi * ((k[:, :, None] * n2[None, None, :]) % SEQ) / SEQ
    t12 = np.concatenate([np.cos(ang2), np.sin(ang2)], axis=2) * 2.0 ** -3.5
    as_bf16 = lambda a: jnp.asarray(a, dtype=F32).astype(BF16)
    return as_bf16(cs), as_bf16(m1), as_bf16(t12)


def kernel(x, c, norm_g, ada_w, ada_b, attn_w_in, attn_q_gain, attn_k_gain, attn_w_out,
           fourier_w_in, fourier_w_out, final_g):
    b, s, d = x.shape
    t = b * s

    mod = _adaln(c.T, ada_w, ada_b)
    shift = mod[:, :, None, :d]
    scale = mod[:, :, None, d:2 * d]
    gate = mod[:, :, None, 2 * d:]

    q_gain = attn_q_gain[0] * (math.log2(math.e) / math.sqrt(HEAD_DIM))
    k_gain = attn_k_gain[0]
    gain_vec = jnp.concatenate([jnp.tile(q_gain, N_HEADS), jnp.tile(k_gain, N_KV_HEADS)])[None, :]
    cos_il, sin_a, sin_b = _rope_tables()

    x2d = x.reshape(t, d)
    qt, k, vt, sg0 = _attn_inproj(x2d, norm_g[0][None, :], scale[0], shift[0],
                                  attn_w_in[0].astype(BF16), gain_vec, cos_il, sin_a, sin_b)
    score_bound = (HEAD_DIM * BF16_NORM_SLACK * jnp.max(jnp.abs(q_gain)) * jnp.max(jnp.abs(k_gain)))
    bounded = (score_bound <= SCORE_BOUND_LOG2).astype(jnp.int32).reshape(1)
    og = _attention(bounded, qt, k, vt, sg0)
    x1 = _outproj_residual(og.reshape(t, ATTN_WIDTH), attn_w_out[0].astype(BF16), x2d, gate[0])

    cs, m1, t12 = _dft_tables()
    z, sg = _fourier_inproj(x1, norm_g[1][None, :], scale[1], shift[1],
                            fourier_w_in[0].astype(BF16), cs)
    fgate = _seq_dft_gate(z, sg, m1, t12)
    out = _fourier_outproj_final(fgate.reshape(t, d), fourier_w_out[0].astype(BF16), x1,
                                 gate[1], final_g[None, :])
    return out.reshape(b, s, d)
```
